```python
import functools
import jax, jax.numpy as jnp
from jax import lax
import numpy as np

D_MODEL = 1024
BATCH = 8
SEQ = 4096
DEPTH = 1
DEC_BATCH = 128
DEC_SEQ = 8
PAST_LEN = 16384
PAGE_SIZE = 128

MLA_HEADS = 8
QK_NOPE = 64
QK_ROPE = 32
V_DIM = 64
Q_LORA = 384
KV_LORA = 256
ROPE_THETA = 10000.0
SM_SCALE = (QK_NOPE + QK_ROPE) ** -0.5
Q_BLOCK = 128

RW_HEADS = 8
RW_HEAD = 64
RW_DIM = RW_HEADS * RW_HEAD
DECAY_LORA = 64
AAA_LORA = 64
GATE_LORA = 128
GN_EPS = RW_HEAD * 1e-5

D_FF = 2816
RMS_EPS = 1e-6
NEG_INF = -1e30

MLA_COLS = Q_LORA + KV_LORA + QK_ROPE
RW_COLS = 3 * RW_DIM + DECAY_LORA + AAA_LORA + GATE_LORA
GATE_COLS = 2 * D_MODEL
PROJ_COLS = MLA_COLS + RW_COLS + GATE_COLS

kernel_name = 'mla_rwkv7_macaron_hybrid_step'


def split_cols(u, sizes):
    out, off = [], 0
    for s in sizes:
        out.append(u[..., off:off + s])
        off += s
    return out


def rmsnorm(x, g):
    xf = x.astype(jnp.float32)
    y = xf * lax.rsqrt(jnp.mean(xf * xf, axis=-1, keepdims=True) + RMS_EPS)
    return (y * g.astype(jnp.float32)).astype(x.dtype)


def swiglu(h, wg, wu, wd):
    return (jax.nn.silu(h @ wg) * (h @ wu)) @ wd


def rope_cos_sin(pos):
    inv = 1.0 / (ROPE_THETA ** (jnp.arange(0, QK_ROPE, 2, dtype=jnp.float32) / QK_ROPE))
    ang = pos.astype(jnp.float32)[:, None] * inv[None, :]
    return jnp.cos(ang), jnp.sin(ang)


def apply_rope(x, cos, sin):
    half = QK_ROPE // 2
    x1 = x[..., :half].astype(jnp.float32)
    x2 = x[..., half:].astype(jnp.float32)
    return jnp.concatenate([x1 * cos - x2 * sin, x2 * cos + x1 * sin], axis=-1).astype(x.dtype)


def mla_features(u_mla, pos, lp):
    bsz, t = u_mla.shape[:2]
    qa, ckv, kpe = split_cols(u_mla, [Q_LORA, KV_LORA, QK_ROPE])
    q = (rmsnorm(qa, lp['q_norm']) @ lp['w_qb']).reshape(bsz, t, MLA_HEADS, QK_NOPE + QK_ROPE)
    q_nope, q_pe = q[..., :QK_NOPE], q[..., QK_NOPE:]
    cos, sin = rope_cos_sin(pos)
    q_pe = apply_rope(q_pe, cos[:, None, :], sin[:, None, :])
    kpe = apply_rope(kpe, cos, sin)
    ckv = rmsnorm(ckv, lp['kv_norm'])
    q_lat = jnp.einsum('bthd,chd->bthc', q_nope, lp['w_uk'])
    return q_lat, q_pe, ckv, kpe


def mla_prompt_attention(q_lat, q_pe, ckv, kpe):
    bsz, t = q_lat.shape[:2]
    nb = t // Q_BLOCK
    ql_b = q_lat.reshape(bsz, nb, Q_BLOCK, MLA_HEADS, KV_LORA).transpose(1, 0, 2, 3, 4)
    qp_b = q_pe.reshape(bsz, nb, Q_BLOCK, MLA_HEADS, QK_ROPE).transpose(1, 0, 2, 3, 4)
    starts = jnp.arange(nb, dtype=jnp.int32) * Q_BLOCK
    kpos = jnp.arange(t, dtype=jnp.int32)

    def block(args):
        ql, qp, start = args
        s = jnp.einsum('bqhc,bkc->bhqk', ql, ckv) + jnp.einsum('bqhr,bkr->bhqk', qp, kpe)
        s = s.astype(jnp.float32) * SM_SCALE
        qpos = start + jnp.arange(Q_BLOCK, dtype=jnp.int32)
        s = jnp.where(kpos[None, :] <= qpos[:, None], s, NEG_INF)
        p = jax.nn.softmax(s, axis=-1).astype(ckv.dtype)
        return jnp.einsum('bhqk,bkc->bqhc', p, ckv)

    o = lax.map(block, (ql_b, qp_b, starts))
    return o.transpose(1, 0, 2, 3, 4).reshape(bsz, t, MLA_HEADS, KV_LORA)


def mla_sample_attention(cache_c, cache_k, page_table, q_lat, q_pe, ckv, kpe):
    t = q_lat.shape[1]
    past = page_table.shape[1] * cache_c.shape[1]
    causal = jnp.tril(jnp.ones((t, t), dtype=bool))

    def one(args):
        ql, qp, cn, kn, pages = args
        c_past = cache_c[pages].reshape(past, KV_LORA)
        k_past = cache_k[pages].reshape(past, QK_ROPE)
        s_past = jnp.einsum('qhc,kc->hqk', ql, c_past) + jnp.einsum('qhr,kr->hqk', qp, k_past)
        s_new = jnp.einsum('qhc,kc->hqk', ql, cn) + jnp.einsum('qhr,kr->hqk', qp, kn)
        s_new = jnp.where(causal[None], s_new.astype(jnp.float32) * SM_SCALE, NEG_INF)
        s = jnp.concatenate([s_past.astype(jnp.float32) * SM_SCALE, s_new], axis=-1)
        p = jax.nn.softmax(s, axis=-1).astype(cn.dtype)
        return (jnp.einsum('hqk,kc->qhc', p[..., :past], c_past)
                + jnp.einsum('hqk,kc->qhc', p[..., past:], cn))

    return lax.map(one, (q_lat, q_pe, ckv, kpe, page_table))


def rwkv_features(us, lp):
    bsz, t = us.shape[:2]
    r, k, v, wd, ad, gd = split_cols(us, [RW_DIM, RW_DIM, RW_DIM, DECAY_LORA, AAA_LORA, GATE_LORA])
    w = -jax.nn.softplus(-(lp['rw_w0'] + jnp.tanh(wd) @ lp['rw_w2'])) - 0.5
    decay = jnp.exp(-jnp.exp(w.astype(jnp.float32)))
    a = jax.nn.sigmoid(lp['rw_a0'] + ad @ lp['rw_a2'])
    g = jax.nn.sigmoid(gd) @ lp['rw_g2']

    def heads(z):
        return z.astype(jnp.float32).reshape(bsz, t, RW_HEADS, RW_HEAD)

    kk = heads(k * lp['rw_kk'])
    kk = kk / jnp.maximum(jnp.sqrt(jnp.sum(kk * kk, axis=-1, keepdims=True)), 1e-12)
    k = k * (1.0 + (a - 1.0) * lp['rw_ka'])
    return heads(r), heads(k), heads(v), heads(decay), kk, heads(a), g


def rwkv_scan(r, k, v, decay, kk, a, s0):
    def step(s, inp):
        r_t, k_t, v_t, d_t, kk_t, a_t = inp
        sa = jnp.einsum('bhij,bhj->bhi', s, -kk_t)
        s = (s * d_t[:, :, None, :] + sa[..., :, None] * (kk_t * a_t)[..., None, :]
             + v_t[..., :, None] * k_t[..., None, :])
        return s, jnp.einsum('bhij,bhj->bhi', s, r_t)

    xs = tuple(jnp.moveaxis(z, 1, 0) for z in (r, k, v, decay, kk, a))
    s_fin, ys = lax.scan(step, s0.astype(jnp.float32), xs)
    return jnp.moveaxis(ys, 0, 1), s_fin


def rwkv_out(y, r, k, v, g, lp):
    bsz, t = y.shape[:2]
    mu = jnp.mean(y, axis=-1, keepdims=True)
    var = jnp.mean(jnp.square(y - mu), axis=-1, keepdims=True)
    yn = ((y - mu) * lax.rsqrt(var + GN_EPS)).reshape(bsz, t, RW_DIM)
    yn = yn * lp['rw_ln_w'].astype(jnp.float32) + lp['rw_ln_b'].astype(jnp.float32)
    bonus = (jnp.sum(r * k * lp['rw_rk'].astype(jnp.float32), axis=-1, keepdims=True) * v).reshape(bsz, t, RW_DIM)
    o = ((yn + bonus) * g.astype(jnp.float32)).astype(g.dtype)
    return o @ lp['w_o_rw']


def layer_forward(x, pos, attend, shift_last, s0, lp):
    bsz, t = x.shape[:2]
    x = x + 0.5 * swiglu(rmsnorm(x, lp['ffn1_norm']), lp['ffn1_wg'], lp['ffn1_wu'], lp['ffn1_wd'])
    h = rmsnorm(x, lp['mix_norm'])
    u_mla, u_rw, u_gate = split_cols(h @ lp['w_in'], [MLA_COLS, RW_COLS, GATE_COLS])
    q_lat, q_pe, ckv, kpe = mla_features(u_mla, pos, lp)
    o_lat = attend(q_lat, q_pe, ckv, kpe)
    o_a = jnp.einsum('bthc,chd->bthd', o_lat, lp['w_uv']).reshape(bsz, t, MLA_HEADS * V_DIM) @ lp['w_o_mla']
    u_prev = jnp.concatenate([shift_last[:, None, :].astype(u_rw.dtype), u_rw[:, :-1]], axis=1)
    us = u_rw + (u_prev - u_rw) * lp['rw_mu']
    r, k, v, decay, kk, a, g = rwkv_features(us, lp)
    y, s_new = rwkv_scan(r, k, v, decay, kk, a, s0)
    o_b = rwkv_out(y, r, k, v, g, lp)
    g_a, g_b = split_cols(u_gate, [D_MODEL, D_MODEL])
    x = x + (jax.nn.sigmoid(g_a) * o_a + jax.nn.sigmoid(g_b) * o_b) @ lp['w_out']
    x = x + 0.5 * swiglu(rmsnorm(x, lp['ffn2_norm']), lp['ffn2_wg'], lp['ffn2_wu'], lp['ffn2_wd'])
    return x, ckv, kpe, s_new.astype(s0.dtype), u_rw[:, -1]


def setup_inputs(seed: int = 0) -> dict:
    key = jax.random.key(seed)
    ks = jax.random.split(key, 40)
    f32 = jnp.float32

    def nrm(i, shape, scale):
        return jax.random.normal(ks[i], shape, f32) * scale

    def gain(i, n):
        return 1.0 + nrm(i, (DEPTH, n), 0.01)

    n_pages = PAST_LEN // PAGE_SIZE
    n_used = DEC_BATCH * n_pages
    n_pool = n_used + n_used // 4
    page_table = jax.random.permutation(ks[4], n_pool)[:n_used].reshape(DEC_BATCH, n_pages).astype(jnp.int32)
    return {
        'x_prompt': nrm(0, (BATCH, SEQ, D_MODEL), 1.0),
        'x_sample': nrm(1, (DEC_BATCH, DEC_SEQ, D_MODEL), 1.0),
        'cache_ckv': nrm(2, (DEPTH, n_pool, PAGE_SIZE, KV_LORA), 1.0),
        'cache_kpe': nrm(3, (DEPTH, n_pool, PAGE_SIZE, QK_ROPE), 1.0),
        'state_wkv': nrm(5, (DEPTH, DEC_BATCH, RW_HEADS, RW_HEAD, RW_HEAD), 0.3),
        'state_shift': nrm(6, (DEPTH, DEC_BATCH, RW_COLS), 1.0),
        'page_table': page_table,
        'ffn1_norm': gain(7, D_MODEL),
        'ffn1_wg': nrm(8, (DEPTH, D_MODEL, D_FF), D_MODEL ** -0.5),
        'ffn1_wu': nrm(9, (DEPTH, D_MODEL, D_FF), D_MODEL ** -0.5),
        'ffn1_wd': nrm(10, (DEPTH, D_FF, D_MODEL), D_FF ** -0.5),
        'mix_norm': gain(11, D_MODEL),
        'w_in': nrm(12, (DEPTH, D_MODEL, PROJ_COLS), D_MODEL ** -0.5),
        'q_norm': gain(13, Q_LORA),
        'kv_norm': gain(14, KV_LORA),
        'w_qb': nrm(15, (DEPTH, Q_LORA, MLA_HEADS * (QK_NOPE + QK_ROPE)), Q_LORA ** -0.5),
        'w_uk': nrm(16, (DEPTH, KV_LORA, MLA_HEADS, QK_NOPE), KV_LORA ** -0.5),
        'w_uv': nrm(17, (DEPTH, KV_LORA, MLA_HEADS, V_DIM), KV_LORA ** -0.5),
        'w_o_mla': nrm(18, (DEPTH, MLA_HEADS * V_DIM, D_MODEL), (MLA_HEADS * V_DIM) ** -0.5),
        'rw_mu': jax.random.uniform(ks[19], (DEPTH, RW_COLS), f32),
        'rw_w0': -1.0 + nrm(20, (DEPTH, RW_DIM), 0.5),
        'rw_w2': nrm(21, (DEPTH, DECAY_LORA, RW_DIM), 0.5 * DECAY_LORA ** -0.5),
        'rw_a0': nrm(22, (DEPTH, RW_DIM), 0.1),
        'rw_a2': nrm(23, (DEPTH, AAA_LORA, RW_DIM), AAA_LORA ** -0.5),
        'rw_g2': nrm(24, (DEPTH, GATE_LORA, RW_DIM), GATE_LORA ** -0.5),
        'rw_kk': 0.85 + nrm(25, (DEPTH, RW_DIM), 0.05),
        'rw_ka': 1.0 + nrm(26, (DEPTH, RW_DIM), 0.05),
        'rw_rk': nrm(27, (DEPTH, RW_HEADS, RW_HEAD), 0.1),
        'rw_ln_w': gain(28, RW_DIM),
        'rw_ln_b': nrm(29, (DEPTH, RW_DIM), 0.01),
        'w_o_rw': nrm(30, (DEPTH, RW_DIM, D_MODEL), RW_DIM ** -0.5),
        'w_out': nrm(31, (DEPTH, D_MODEL, D_MODEL), D_MODEL ** -0.5),
        'ffn2_norm': gain(32, D_MODEL),
        'ffn2_wg': nrm(33, (DEPTH, D_MODEL, D_FF), D_MODEL ** -0.5),
        'ffn2_wu': nrm(34, (DEPTH, D_MODEL, D_FF), D_MODEL ** -0.5),
        'ffn2_wd': nrm(35, (DEPTH, D_FF, D_MODEL), D_FF ** -0.5),
        'final_norm': 1.0 + nrm(36, (D_MODEL,), 0.01),
    }


def reference(x_prompt, x_sample, cache_ckv, cache_kpe, state_wkv, state_shift, page_table,
              ffn1_norm, ffn1_wg, ffn1_wu, ffn1_wd, mix_norm, w_in, q_norm, kv_norm, w_qb, w_uk, w_uv,
              w_o_mla, rw_mu, rw_w0, rw_w2, rw_a0, rw_a2, rw_g2, rw_kk, rw_ka, rw_rk, rw_ln_w, rw_ln_b,
              w_o_rw, w_out, ffn2_norm, ffn2_wg, ffn2_wu, ffn2_wd, final_norm):
    past_len = page_table.shape[1] * cache_ckv.shape[2]
    pos_p = jnp.arange(x_prompt.shape[1], dtype=jnp.int32)
    pos_s = past_len + jnp.arange(x_sample.shape[1], dtype=jnp.int32)
    bsz_p = x_prompt.shape[0]
    xp, xs = x_prompt, x_sample
    ckv_p, kpe_p, wkv_p, sh_p = [], [], [], []
    ckv_s, kpe_s, wkv_s, sh_s = [], [], [], []
    for l in range(DEPTH):
        lp = dict(ffn1_norm=ffn1_norm[l], ffn1_wg=ffn1_wg[l], ffn1_wu=ffn1_wu[l], ffn1_wd=ffn1_wd[l],
                  mix_norm=mix_norm[l], w_in=w_in[l], q_norm=q_norm[l], kv_norm=kv_norm[l], w_qb=w_qb[l],
                  w_uk=w_uk[l], w_uv=w_uv[l], w_o_mla=w_o_mla[l], rw_mu=rw_mu[l], rw_w0=rw_w0[l],
                  rw_w2=rw_w2[l], rw_a0=rw_a0[l], rw_a2=rw_a2[l], rw_g2=rw_g2[l], rw_kk=rw_kk[l],
                  rw_ka=rw_ka[l], rw_rk=rw_rk[l], rw_ln_w=rw_ln_w[l], rw_ln_b=rw_ln_b[l], w_o_rw=w_o_rw[l],
                  w_out=w_out[l], ffn2_norm=ffn2_norm[l], ffn2_wg=ffn2_wg[l], ffn2_wu=ffn2_wu[l],
                  ffn2_wd=ffn2_wd[l])
        shift0 = jnp.zeros((bsz_p, RW_COLS), xp.dtype)
        s0 = jnp.zeros((bsz_p, RW_HEADS, RW_HEAD, RW_HEAD), state_wkv.dtype)
        xp, c1, k1, w1, h1 = layer_forward(xp, pos_p, mla_prompt_attention, shift0, s0, lp)
        ckv_p.append(c1); kpe_p.append(k1); wkv_p.append(w1); sh_p.append(h1)
        attend_s = functools.partial(mla_sample_attention, cache_ckv[l], cache_kpe[l], page_table)
        xs, c2, k2, w2, h2 = layer_forward(xs, pos_s, attend_s, state_shift[l], state_wkv[l], lp)
        ckv_s.append(c2); kpe_s.append(k2); wkv_s.append(w2); sh_s.append(h2)
    y_prompt = rmsnorm(xp, final_norm)
    y_sample = rmsnorm(xs, final_norm)
    return (y_prompt, y_sample,
            jnp.stack(ckv_p), jnp.stack(kpe_p), jnp.stack(wkv_p), jnp.stack(sh_p),
            jnp.stack(ckv_s), jnp.stack(kpe_s), jnp.stack(wkv_s), jnp.stack(sh_s))
```

```python
import functools

import jax
import jax.numpy as jnp
from jax import lax
from jax.experimental import pallas as pl
from jax.experimental.pallas import tpu as pltpu

F32 = jnp.float32
BF16 = jnp.bfloat16

D_MODEL = 1024
MLA_HEADS = 8
QK_NOPE = 64
QK_ROPE = 32
V_DIM = 64
Q_LORA = 384
KV_LORA = 256
ROPE_THETA = 10000.0
SM_SCALE = (QK_NOPE + QK_ROPE) ** -0.5
RW_HEADS = 8
RW_HEAD = 64
RW_DIM = RW_HEADS * RW_HEAD
DECAY_LORA = 64
AAA_LORA = 64
GATE_LORA = 128
GN_EPS = RW_HEAD * 1e-5
D_FF = 2816
RMS_EPS = 1e-6
NEG_INF = -1e30
MLA_COLS = Q_LORA + KV_LORA + QK_ROPE
RW_COLS = 3 * RW_DIM + DECAY_LORA + AAA_LORA + GATE_LORA

LANES = 128
ROPE_PAD = LANES
QK_WIDTH = KV_LORA + ROPE_PAD
FF_CHUNKS = 2
VMEM_LIMIT = 56 * 1024 * 1024

_NT = (((1,), (1,)), ((), ()))
_TN = (((0,), (0,)), ((), ()))


def _pick_tile(n, pref):
    t = pref
    while n % t:
        t //= 2
    return t


def _const_spec(shape):
    nd = len(shape)
    return pl.BlockSpec(shape, lambda *_: (0,) * nd, pipeline_mode=pl.Buffered(1))


def _params(sem):
    return pltpu.CompilerParams(dimension_semantics=sem, vmem_limit_bytes=VMEM_LIMIT)


def _dot(a, b):
    return jnp.dot(a, b, preferred_element_type=F32)


def _dot_nt(a, b):
    return lax.dot_general(a, b, _NT, preferred_element_type=F32)


def _dot_tn(a, b):
    return lax.dot_general(a, b, _TN, preferred_element_type=F32)


def _rmsnorm(x, g):
    return x * lax.rsqrt(jnp.mean(x * x, axis=-1, keepdims=True) + RMS_EPS) * g


def _swiglu(hn, wg_ref, wu_ref, wd_ref):
    fc = D_FF // FF_CHUNKS
    acc = None
    for c in range(FF_CHUNKS):
        g = _dot(hn, wg_ref[:, c * fc:(c + 1) * fc])
        u = _dot(hn, wu_ref[:, c * fc:(c + 1) * fc])
        act = (g * jax.nn.sigmoid(g) * u).astype(BF16)
        part = _dot(act, wd_ref[c * fc:(c + 1) * fc, :])
        acc = part if acc is None else acc + part
    return acc


def _ffn1_kernel(x_ref, n1_ref, wg_ref, wu_ref, wd_ref, nm_ref, x1_ref, h_ref):
    x = x_ref[...]
    hn = _rmsnorm(x, n1_ref[...]).astype(BF16)
    x1 = x + 0.5 * _swiglu(hn, wg_ref, wu_ref, wd_ref)
    x1_ref[...] = x1
    h_ref[...] = _rmsnorm(x1, nm_ref[...]).astype(BF16)


def _ffn1(x, n1, wg, wu, wd, nm):
    n = x.shape[0]
    tm = _pick_tile(n, 512)
    row = lambda w: pl.BlockSpec((tm, w), lambda i: (i, 0))
    return pl.pallas_call(
        _ffn1_kernel,
        grid=(n // tm,),
        in_specs=[row(D_MODEL), _const_spec((1, D_MODEL)), _const_spec((D_MODEL, D_FF)),
                  _const_spec((D_MODEL, D_FF)), _const_spec((D_FF, D_MODEL)), _const_spec((1, D_MODEL))],
        out_specs=[row(D_MODEL), row(D_MODEL)],
        out_shape=[jax.ShapeDtypeStruct((n, D_MODEL), F32), jax.ShapeDtypeStruct((n, D_MODEL), BF16)],
        compiler_params=_params(("parallel",)),
        name="ffn1",
    )(x, n1, wg, wu, wd, nm)


def _proj_kernel(h_ref, cos_ref, sin_ref, wmla_ref, wrw_ref, wgate_ref, qn_ref, kvn_ref,
                 wnope_ref, wpe_ref, wper_ref, wuk_ref,
                 q_ref, kv_ref, ckv_ref, kpe_ref, urw_ref, gate_ref):
    h = h_ref[...]
    cos = cos_ref[...]
    sin = sin_ref[...]
    u = _dot(h, wmla_ref[...])
    qa = u[:, :Q_LORA]
    c0 = Q_LORA + KV_LORA
    ckv = _rmsnorm(u[:, Q_LORA:c0], kvn_ref[...])
    kpe = u[:, c0:c0 + ROPE_PAD] * cos + u[:, c0 + ROPE_PAD:] * sin
    ckv_ref[...] = ckv
    kpe_ref[...] = kpe[:, :QK_ROPE]
    kv_ref[...] = jnp.concatenate([ckv, kpe], axis=1).astype(BF16)

    qn = _rmsnorm(qa, qn_ref[...]).astype(BF16)
    q_nope = _dot(qn, wnope_ref[...]).astype(BF16)
    q_lat = _dot(q_nope, wuk_ref[...])
    qp = _dot(qn, wpe_ref[...])
    qr = _dot(qn, wper_ref[...])
    for hh in range(MLA_HEADS):
        ql = q_lat[:, hh * KV_LORA:(hh + 1) * KV_LORA] * SM_SCALE
        sl = slice(hh * ROPE_PAD, (hh + 1) * ROPE_PAD)
        pe = (qp[:, sl] * cos + qr[:, sl] * sin) * SM_SCALE
        q_ref[hh] = jnp.concatenate([ql, pe], axis=1).astype(BF16)

    urw_ref[...] = _dot(h, wrw_ref[...])
    gate_ref[...] = jax.nn.sigmoid(_dot(h, wgate_ref[...]))


def _proj(h, cos, sin, w, n_rope_blocks):
    n = h.shape[0]
    tm = _pick_tile(cos.shape[0], _pick_tile(n, 512))
    nrb = cos.shape[0] // tm
    row = lambda wd_: pl.BlockSpec((tm, wd_), lambda i: (i, 0))
    rope = pl.BlockSpec((tm, ROPE_PAD), lambda i: (i % nrb, 0))
    wmla_w = Q_LORA + KV_LORA + 2 * ROPE_PAD
    return pl.pallas_call(
        _proj_kernel,
        grid=(n // tm,),
        in_specs=[row(D_MODEL), rope, rope,
                  _const_spec((D_MODEL, wmla_w)), _const_spec((D_MODEL, RW_COLS)),
                  _const_spec((D_MODEL, 2 * D_MODEL)), _const_spec((1, Q_LORA)), _const_spec((1, KV_LORA)),
                  _const_spec((Q_LORA, MLA_HEADS * QK_NOPE)), _const_spec((Q_LORA, MLA_HEADS * ROPE_PAD)),
                  _const_spec((Q_LORA, MLA_HEADS * ROPE_PAD)),
                  _const_spec((MLA_HEADS * QK_NOPE, MLA_HEADS * KV_LORA))],
        out_specs=[pl.BlockSpec((MLA_HEADS, tm, QK_WIDTH), lambda i: (0, i, 0)),
                   row(QK_WIDTH), row(KV_LORA), row(QK_ROPE), row(RW_COLS), row(2 * D_MODEL)],
        out_shape=[jax.ShapeDtypeStruct((MLA_HEADS, n, QK_WIDTH), BF16),
                   jax.ShapeDtypeStruct((n, QK_WIDTH), BF16),
                   jax.ShapeDtypeStruct((n, KV_LORA), F32),
                   jax.ShapeDtypeStruct((n, QK_ROPE), F32),
                   jax.ShapeDtypeStruct((n, RW_COLS), F32),
                   jax.ShapeDtypeStruct((n, 2 * D_MODEL), F32)],
        compiler_params=_params(("parallel",)),
        name="proj",
    )(h, cos, sin, w["w_mla"], w["w_rw"], w["w_gate"], w["q_norm"], w["kv_norm"],
      w["w_nope"], w["w_pe"], w["w_pe_rot"], w["w_uk_bd"])


def _softmax_update(s, v, m_ref, l_ref, acc_ref):
    m_prev = m_ref[...]
    m_new = jnp.maximum(m_prev, jnp.max(s, axis=-1, keepdims=True))
    alpha = jnp.exp(m_prev - m_new)
    p = jnp.exp(s - m_new)
    l_ref[...] = alpha * l_ref[...] + jnp.sum(p, axis=-1, keepdims=True)
    acc_ref[...] = alpha * acc_ref[...] + _dot(p.astype(BF16), v)
    m_ref[...] = m_new


def _mla_out(o_lat, wuv_ref, wo_ref, rows):
    out = None
    for hh in range(MLA_HEADS):
        oh = _dot(o_lat[hh * rows:(hh + 1) * rows].astype(BF16), wuv_ref[hh]).astype(BF16)
        part = _dot(oh, wo_ref[hh])
        out = part if out is None else out + part
    return out


def _attn_prompt_kernel(q_ref, kv_ref, wuv_ref, wo_ref, o_ref, m_ref, l_ref, acc_ref, *, tq, tk):
    i = pl.program_id(1)
    rows = MLA_HEADS * tq
    q = q_ref[...].reshape(rows, QK_WIDTH)
    m_ref[...] = jnp.full(m_ref.shape, NEG_INF, F32)
    l_ref[...] = jnp.zeros(l_ref.shape, F32)
    acc_ref[...] = jnp.zeros(acc_ref.shape, F32)
    n_full = (i * tq) // tk

    def full_step(j, carry):
        kv = kv_ref[pl.ds(pl.multiple_of(j * tk, tk), tk), :]
        _softmax_update(_dot_nt(q, kv), kv[:, :KV_LORA], m_ref, l_ref, acc_ref)
        return carry

    lax.fori_loop(0, n_full, full_step, 0)

    k0 = pl.multiple_of(n_full * tk, tk)
    kv = kv_ref[pl.ds(k0, tk), :]
    s = _dot_nt(q, kv)
    qpos = i * tq + lax.broadcasted_iota(jnp.int32, (rows, tk), 0) % tq
    kpos = k0 + lax.broadcasted_iota(jnp.int32, (rows, tk), 1)
    s = jnp.where(kpos <= qpos, s, NEG_INF)
    _softmax_update(s, kv[:, :KV_LORA], m_ref, l_ref, acc_ref)

    o_lat = acc_ref[...] / l_ref[...]
    o_ref[...] = _mla_out(o_lat, wuv_ref, wo_ref, tq)


def _attn_prompt(q, kv, wuv, wo, bsz, t):
    tq = _pick_tile(t, 128)
    tk = _pick_tile(t, 512)
    nq = t // tq
    rows = MLA_HEADS * tq
    kern = functools.partial(_attn_prompt_kernel, tq=tq, tk=tk)
    return pl.pallas_call(
        kern,
        grid=(bsz, nq),
        in_specs=[pl.BlockSpec((MLA_HEADS, tq, QK_WIDTH), lambda b, i: (0, b * nq + i, 0)),
                  pl.BlockSpec((t, QK_WIDTH), lambda b, i: (b, 0)),
                  _const_spec((MLA_HEADS, KV_LORA, V_DIM)), _const_spec((MLA_HEADS, V_DIM, D_MODEL))],
        out_specs=pl.BlockSpec((tq, D_MODEL), lambda b, i: (b * nq + i, 0)),
        out_shape=jax.ShapeDtypeStruct((bsz * t, D_MODEL), F32),
        scratch_shapes=[pltpu.VMEM((rows, 1), F32), pltpu.VMEM((rows, 1), F32),
                        pltpu.VMEM((rows, KV_LORA), F32)],
        compiler_params=_params(("parallel", "arbitrary")),
        name="attn_prompt",
    )(q, kv, wuv, wo)


def _attn_sample_kernel(pt_ref, q_ref, kvn_ref, *refs, pg, tnew):
    c_refs = refs[:pg]
    k_refs = refs[pg:2 * pg]
    o_ref, m_ref, l_ref, acc_ref = refs[2 * pg:]
    s_idx = pl.program_id(1)

    @pl.when(s_idx == 0)
    def _():
        m_ref[...] = jnp.full(m_ref.shape, NEG_INF, F32)
        l_ref[...] = jnp.zeros(l_ref.shape, F32)
        acc_ref[...] = jnp.zeros(acc_ref.shape, F32)

    q = q_ref[...]
    c = jnp.concatenate([r[...] for r in c_refs], axis=0).astype(BF16)
    kp = jnp.concatenate([r[...] for r in k_refs], axis=0).astype(BF16)
    s = _dot_nt(q[:, :KV_LORA], c) + _dot_nt(q[:, KV_LORA:KV_LORA + QK_ROPE], kp)
    _softmax_update(s, c, m_ref, l_ref, acc_ref)

    @pl.when(s_idx == pl.num_programs(1) - 1)
    def _():
        kvn = kvn_ref[...].astype(BF16)
        sn = _dot_nt(q, kvn)
        rows = q.shape[0]
        qt = lax.broadcasted_iota(jnp.int32, (rows, tnew), 0) % tnew
        kt = lax.broadcasted_iota(jnp.int32, (rows, tnew), 1)
        sn = jnp.where(kt <= qt, sn, NEG_INF)
        _softmax_update(sn, kvn[:, :KV_LORA], m_ref, l_ref, acc_ref)
        o_ref[...] = acc_ref[...] / l_ref[...]


def _attn_sample(page_table, q, kvn, cache_c, cache_k, pg):
    nb, n_pages = page_table.shape
    tnew = kvn.shape[1]
    rows = q.shape[1]
    page = cache_c.shape[1]
    steps = n_pages // pg
    kern = functools.partial(_attn_sample_kernel, pg=pg, tnew=tnew)

    def page_spec(width, g):
        return pl.BlockSpec((None, page, width), lambda b, s, pt: (pt[b * n_pages + s * pg + g], 0, 0))

    grid_spec = pltpu.PrefetchScalarGridSpec(
        num_scalar_prefetch=1,
        grid=(nb, steps),
        in_specs=[pl.BlockSpec((None, rows, QK_WIDTH), lambda b, s, pt: (b, 0, 0)),
                  pl.BlockSpec((None, tnew, QK_WIDTH), lambda b, s, pt: (b, 0, 0))]
                 + [page_spec(KV_LORA, g) for g in range(pg)]
                 + [page_spec(QK_ROPE, g) for g in range(pg)],
        out_specs=pl.BlockSpec((None, rows, KV_LORA), lambda b, s, pt: (b, 0, 0)),
        scratch_shapes=[pltpu.VMEM((rows, 1), F32), pltpu.VMEM((rows, 1), F32),
                        pltpu.VMEM((rows, KV_LORA), F32)],
    )
    return pl.pallas_call(
        kern,
        grid_spec=grid_spec,
        out_shape=jax.ShapeDtypeStruct((nb, rows, KV_LORA), F32),
        compiler_params=_params(("parallel", "arbitrary")),
        name="attn_sample",
    )(page_table.reshape(-1), q, kvn, *([cache_c] * pg), *([cache_k] * pg))


def _mla_out_kernel(ol_ref, wuv_ref, wo_ref, o_ref):
    tm = ol_ref.shape[1]
    o_ref[...] = _mla_out(ol_ref[...].reshape(MLA_HEADS * tm, KV_LORA), wuv_ref, wo_ref, tm)


def _mla_out_call(o_lat, wuv, wo):
    n = o_lat.shape[1]
    tm = _pick_tile(n, 256)
    return pl.pallas_call(
        _mla_out_kernel,
        grid=(n // tm,),
        in_specs=[pl.BlockSpec((MLA_HEADS, tm, KV_LORA), lambda i: (0, i, 0)),
                  _const_spec((MLA_HEADS, KV_LORA, V_DIM)), _const_spec((MLA_HEADS, V_DIM, D_MODEL))],
        out_specs=pl.BlockSpec((tm, D_MODEL), lambda i: (i, 0)),
        out_shape=jax.ShapeDtypeStruct((n, D_MODEL), F32),
        compiler_params=_params(("parallel",)),
        name="mla_out",
    )(o_lat, wuv, wo)


def _split_dot(x, w_bf, pieces):
    out = None
    rem = x
    for _ in range(pieces):
        hi = rem.astype(BF16)
        part = _dot(hi, w_bf)
        out = part if out is None else out + part
        rem = rem - hi.astype(F32)
    return out


def _expand_heads(x, hg):
    lane_head = lax.broadcasted_iota(jnp.int32, x.shape, 1) // RW_HEAD
    return jnp.concatenate([jnp.where(lane_head == hh, x, 0.0) for hh in range(hg)], axis=0)


def _unit_lower_inverse(m, c):
    n = m.shape[0]
    eye = (lax.broadcasted_iota(jnp.int32, (n, n), 0) == lax.broadcasted_iota(jnp.int32, (n, n), 1)).astype(F32)
    p = eye + m
    mk = m
    k = 2
    while k < c:
        mkb = mk.astype(BF16)
        mk = _dot(mkb, mkb)
        p = p + _dot(p.astype(BF16), mk.astype(BF16))
        k *= 2
    return p


def _rwkv_kernel(u_ref, sh_ref, s0_ref, mu_ref, w0_ref, w2a2_ref, a0_ref, g2_ref, kkw_ref, ka_ref, rk_ref,
                 lnw_ref, lnb_ref, ones_ref, wo_ref,
                 ob_ref, sout_ref,
                 state_ref, prev_ref, fr_ref, fk_ref, fv_ref, flw_ref, fkk_ref, fa_ref, y_ref, *, c, hg):
    t = pl.program_id(1)
    tt = u_ref.shape[0]
    ng = RW_HEADS // hg
    w = hg * RW_HEAD
    r_ = hg * c

    blockmask = (lax.broadcasted_iota(jnp.int32, (w, w), 0) // RW_HEAD
                 == lax.broadcasted_iota(jnp.int32, (w, w), 1) // RW_HEAD)

    @pl.when(t == 0)
    def _():
        prev_ref[...] = sh_ref[...]
        for g in range(ng):
            s0g = s0_ref[:, g * w:(g + 1) * w]
            state_ref[g] = jnp.where(blockmask, jnp.concatenate([s0g] * hg, axis=0), 0.0)

    u = u_ref[...]
    row = lax.broadcasted_iota(jnp.int32, u.shape, 0)
    u_prev = jnp.where(row == 0, prev_ref[...], pltpu.roll(u, 1, 0))
    prev_ref[...] = u[tt - 1:tt, :]
    us = u + (u_prev - u) * mu_ref[...]
    r = us[:, 0:RW_DIM]
    k = us[:, RW_DIM:2 * RW_DIM]
    v = us[:, 2 * RW_DIM:3 * RW_DIM]
    o1 = 3 * RW_DIM
    wa = us[:, o1:o1 + DECAY_LORA + AAA_LORA]
    gd = us[:, o1 + DECAY_LORA + AAA_LORA:]
    lane = lax.broadcasted_iota(jnp.int32, wa.shape, 1)
    wa = jnp.where(lane < DECAY_LORA, jnp.tanh(wa), wa)
    lo = _dot(wa.astype(BF16), w2a2_ref[...])
    w_raw = -jax.nn.softplus(-(w0_ref[...] + lo[:, :RW_DIM])) - 0.5
    logw = -jnp.exp(w_raw)
    a = jax.nn.sigmoid(a0_ref[...] + lo[:, RW_DIM:])
    g = _dot(jax.nn.sigmoid(gd).astype(BF16), g2_ref[...])
    ones_bd = ones_ref[...]
    kk = k * kkw_ref[...]
    kk = kk / jnp.maximum(jnp.sqrt(_split_dot(kk * kk, ones_bd, 2)), 1e-12)
    k2 = k * (1.0 + (a - 1.0) * ka_ref[...])
    fr_ref[...] = r
    fk_ref[...] = k2
    fv_ref[...] = v
    flw_ref[...] = logw
    fkk_ref[...] = kk
    fa_ref[...] = a

    tri = (lax.broadcasted_iota(jnp.int32, (c, c), 1) <= lax.broadcasted_iota(jnp.int32, (c, c), 0)).astype(BF16)
    rr = lax.broadcasted_iota(jnp.int32, (r_, r_), 0)
    cc = lax.broadcasted_iota(jnp.int32, (r_, r_), 1)
    strict_bd = (rr // c == cc // c) & (cc % c < rr % c)
    incl_wide = (lax.broadcasted_iota(jnp.int32, (c, r_), 1) % c
                 <= lax.broadcasted_iota(jnp.int32, (c, r_), 0))

    def chunk(ci, carry):
        off = pl.multiple_of(ci * c, c)
        rows = pl.ds(off, c)
        rc, kc, vc = fr_ref[rows, :], fk_ref[rows, :], fv_ref[rows, :]
        lw, kkc, ac = flw_ref[rows, :], fkk_ref[rows, :], fa_ref[rows, :]
        cs = _split_dot_lhs(tri, lw)
        e_cur = jnp.exp(cs)
        e_inv = jnp.exp(-cs)
        at = -kkc * jnp.exp(cs - lw)
        bt = kkc * ac * e_inv
        kt = kc * e_inv
        rt = rc * e_cur
        p_end = e_cur[c - 1:c, :]
        ys = []
        for g_ in range(ng):
            sl = slice(g_ * w, (g_ + 1) * w)
            a_, b_, k_, r__, v_ = at[:, sl], bt[:, sl], kt[:, sl], rt[:, sl], vc[:, sl]
            ax, bx, kx, vx = (_expand_heads(z, hg) for z in (a_, b_, k_, v_))
            n1 = _dot_nt(jnp.concatenate([ax, r__], axis=0).astype(BF16),
                         jnp.concatenate([bx, kx], axis=0).astype(BF16))
            m_ab = jnp.where(strict_bd, n1[:r_, :r_], 0.0)
            m_ak = jnp.where(strict_bd, n1[:r_, r_:], 0.0)
            a_rb = jnp.where(incl_wide, n1[r_:, :r_], 0.0)
            a_rk = jnp.where(incl_wide, n1[r_:, r_:], 0.0)
            tinv = _unit_lower_inverse(m_ab, c)
            s_f32 = state_ref[g_]
            s_bf = s_f32.astype(BF16)
            vxb = vx.astype(BF16)
            wx = _expand_heads(_dot_nt(a_.astype(BF16), s_bf), hg) + _dot(m_ak.astype(BF16), vxb)
            ux = _dot(tinv.astype(BF16), wx.astype(BF16))
            uxb = ux.astype(BF16)
            u_ = ux[0:c]
            for hh in range(1, hg):
                u_ = u_ + ux[hh * c:(hh + 1) * c]
            y = (_dot_nt(r__.astype(BF16), s_bf)
                 + _dot(jnp.concatenate([a_rb, a_rk], axis=1).astype(BF16),
                        jnp.concatenate([uxb, vxb], axis=0)))
            ds = _dot_tn(jnp.concatenate([u_, v_], axis=0).astype(BF16),
                         jnp.concatenate([b_, k_], axis=0).astype(BF16))
            state_ref[g_] = (s_f32 + jnp.where(blockmask, ds, 0.0)) * p_end[:, sl]
            ys.append(y)
        y_ref[rows, :] = ys[0] if ng == 1 else jnp.concatenate(ys, axis=1)
        return carry

    lax.fori_loop(0, tt // c, chunk, 0)

    y = y_ref[...]
    inv_n = 1.0 / RW_HEAD
    mean = _split_dot(y, ones_bd, 2) * inv_n
    d = y - mean
    var = _split_dot(d * d, ones_bd, 2) * inv_n
    yn = d * lax.rsqrt(var + GN_EPS) * lnw_ref[...] + lnb_ref[...]
    bonus = _split_dot(r * k2 * rk_ref[...], ones_bd, 2) * v
    ob_ref[...] = _dot(((yn + bonus) * g).astype(BF16), wo_ref[...])

    @pl.when(t == pl.num_programs(1) - 1)
    def _():
        for g_ in range(ng):
            s_bd = state_ref[g_]
            s_ = s_bd[0:RW_HEAD]
            for hh in range(1, hg):
                s_ = s_ + s_bd[hh * RW_HEAD:(hh + 1) * RW_HEAD]
            sout_ref[:, g_ * w:(g_ + 1) * w] = s_


def _split_dot_lhs(tri_bf, x):
    out = None
    rem = x
    for _ in range(3):
        hi = rem.astype(BF16)
        part = _dot(tri_bf, hi)
        out = part if out is None else out + part
        rem = rem - hi.astype(F32)
    return out


def _rwkv(u_rw, shift_in, s0, w, bsz, t):
    if t >= 64:
        c, hg = 64, 4
    else:
        c, hg = t, RW_HEADS
    tt = _pick_tile(t, 256)
    nt = t // tt
    ng = RW_HEADS // hg
    wd_ = hg * RW_HEAD
    kern = functools.partial(_rwkv_kernel, c=c, hg=hg)
    vec = lambda n: _const_spec((1, n))
    feat = pltpu.VMEM((tt, RW_DIM), F32)
    return pl.pallas_call(
        kern,
        grid=(bsz, nt),
        in_specs=[pl.BlockSpec((tt, RW_COLS), lambda b, i: (b * nt + i, 0)),
                  pl.BlockSpec((None, 1, RW_COLS), lambda b, i: (b, 0, 0)),
                  pl.BlockSpec((None, RW_HEAD, RW_DIM), lambda b, i: (b, 0, 0)),
                  vec(RW_COLS), vec(RW_DIM), _const_spec((DECAY_LORA + AAA_LORA, 2 * RW_DIM)), vec(RW_DIM),
                  _const_spec((GATE_LORA, RW_DIM)), vec(RW_DIM), vec(RW_DIM), vec(RW_DIM), vec(RW_DIM),
                  vec(RW_DIM), _const_spec((RW_DIM, RW_DIM)), _const_spec((RW_DIM, D_MODEL))],
        out_specs=[pl.BlockSpec((tt, D_MODEL), lambda b, i: (b * nt + i, 0)),
                   pl.BlockSpec((None, RW_HEAD, RW_DIM), lambda b, i: (b, 0, 0))],
        out_shape=[jax.ShapeDtypeStruct((bsz * t, D_MODEL), F32),
                   jax.ShapeDtypeStruct((bsz, RW_HEAD, RW_DIM), F32)],
        scratch_shapes=[pltpu.VMEM((ng, wd_, wd_), F32), pltpu.VMEM((1, RW_COLS), F32),
                        feat, feat, feat, feat, feat, feat, feat],
        compiler_params=_params(("parallel", "arbitrary")),
        name="rwkv",
    )(u_rw, shift_in, s0, w["rw_mu"], w["rw_w0"], w["rw_w2a2"], w["rw_a0"], w["rw_g2"], w["rw_kk"],
      w["rw_ka"], w["rw_rk"], w["rw_ln_w"], w["rw_ln_b"], w["ones_bd"], w["w_o_rw"])


def _merge_kernel(x1_ref, oa_ref, ob_ref, gate_ref, wout_ref, n2_ref, wg_ref, wu_ref, wd_ref, nf_ref, y_ref):
    mixed = (gate_ref[:, :D_MODEL] * oa_ref[...] + gate_ref[:, D_MODEL:] * ob_ref[...]).astype(BF16)
    x2 = x1_ref[...] + _dot(mixed, wout_ref[...])
    hn = _rmsnorm(x2, n2_ref[...]).astype(BF16)
    x3 = x2 + 0.5 * _swiglu(hn, wg_ref, wu_ref, wd_ref)
    y_ref[...] = _rmsnorm(x3, nf_ref[...])


def _merge(x1, oa, ob, gate, wout, n2, wg, wu, wd, nf):
    n = x1.shape[0]
    tm = _pick_tile(n, 256)
    row = lambda w_: pl.BlockSpec((tm, w_), lambda i: (i, 0))
    return pl.pallas_call(
        _merge_kernel,
        grid=(n // tm,),
        in_specs=[row(D_MODEL), row(D_MODEL), row(D_MODEL), row(2 * D_MODEL),
                  _const_spec((D_MODEL, D_MODEL)), _const_spec((1, D_MODEL)), _const_spec((D_MODEL, D_FF)),
                  _const_spec((D_MODEL, D_FF)), _const_spec((D_FF, D_MODEL)), _const_spec((1, D_MODEL))],
        out_specs=row(D_MODEL),
        out_shape=jax.ShapeDtypeStruct((n, D_MODEL), F32),
        compiler_params=_params(("parallel",)),
        name="merge_ffn2",
    )(x1, oa, ob, gate, wout, n2, wg, wu, wd, nf)


def _rope_tables(pos):
    inv = 1.0 / (ROPE_THETA ** (jnp.arange(0, QK_ROPE, 2, dtype=F32) / QK_ROPE))
    ang = pos.astype(F32)[:, None] * inv[None, :]
    pad = jnp.zeros((pos.shape[0], ROPE_PAD - QK_ROPE), F32)
    cos, sin = jnp.cos(ang), jnp.sin(ang)
    return jnp.concatenate([cos, cos, pad], axis=1), jnp.concatenate([sin, sin, pad], axis=1)


def _rot_cols(wm):
    half = QK_ROPE // 2
    return jnp.concatenate([-wm[..., half:], wm[..., :half]], axis=-1)


def _pad_cols(wm, width):
    return jnp.pad(wm, [(0, 0)] * (wm.ndim - 1) + [(0, width - wm.shape[-1])])


def _prep_weights(p):
    w_in = p["w_in"]
    w_kpe = w_in[:, Q_LORA + KV_LORA:MLA_COLS]
    w_mla = jnp.concatenate([w_in[:, :Q_LORA + KV_LORA], _pad_cols(w_kpe, ROPE_PAD),
                             _pad_cols(_rot_cols(w_kpe), ROPE_PAD)], axis=1)
    w_qb = p["w_qb"].reshape(Q_LORA, MLA_HEADS, QK_NOPE + QK_ROPE)
    w_nope = w_qb[:, :, :QK_NOPE].reshape(Q_LORA, MLA_HEADS * QK_NOPE)
    w_pe = w_qb[:, :, QK_NOPE:]
    w_uk = jnp.transpose(p["w_uk"], (1, 2, 0))
    eye = jnp.eye(MLA_HEADS, dtype=F32)
    w_uk_bd = (eye[:, None, :, None] * w_uk[:, :, None, :]).reshape(MLA_HEADS * QK_NOPE, MLA_HEADS * KV_LORA)
    zeros = jnp.zeros((DECAY_LORA, RW_DIM), F32)
    w2a2 = jnp.concatenate([jnp.concatenate([p["rw_w2"], zeros], axis=1),
                            jnp.concatenate([zeros, p["rw_a2"]], axis=1)], axis=0)
    head = jnp.arange(RW_DIM) // RW_HEAD
    row = lambda v: v.reshape(1, -1)
    return {
        "w_mla": w_mla.astype(BF16),
        "w_rw": w_in[:, MLA_COLS:MLA_COLS + RW_COLS].astype(BF16),
        "w_gate": w_in[:, MLA_COLS + RW_COLS:].astype(BF16),
        "q_norm": row(p["q_norm"]), "kv_norm": row(p["kv_norm"]),
        "w_nope": w_nope.astype(BF16),
        "w_pe": _pad_cols(w_pe, ROPE_PAD).reshape(Q_LORA, MLA_HEADS * ROPE_PAD).astype(BF16),
        "w_pe_rot": _pad_cols(_rot_cols(w_pe), ROPE_PAD).reshape(Q_LORA, MLA_HEADS * ROPE_PAD).astype(BF16),
        "w_uk_bd": w_uk_bd.astype(BF16),
        "w_uv": jnp.transpose(p["w_uv"], (1, 0, 2)).astype(BF16),
        "w_o_mla": p["w_o_mla"].reshape(MLA_HEADS, V_DIM, D_MODEL).astype(BF16),
        "rw_mu": row(p["rw_mu"]), "rw_w0": row(p["rw_w0"]), "rw_w2a2": w2a2.astype(BF16),
        "rw_a0": row(p["rw_a0"]), "rw_g2": p["rw_g2"].astype(BF16), "rw_kk": row(p["rw_kk"]),
        "rw_ka": row(p["rw_ka"]), "rw_rk": row(p["rw_rk"]), "rw_ln_w": row(p["rw_ln_w"]),
        "rw_ln_b": row(p["rw_ln_b"]),
        "ones_bd": (head[:, None] == head[None, :]).astype(BF16),
        "w_o_rw": p["w_o_rw"].astype(BF16),
        "w_out": p["w_out"].astype(BF16),
        "ffn1_norm": row(p["ffn1_norm"]), "mix_norm": row(p["mix_norm"]), "ffn2_norm": row(p["ffn2_norm"]),
        "ffn1_wg": p["ffn1_wg"].astype(BF16), "ffn1_wu": p["ffn1_wu"].astype(BF16),
        "ffn1_wd": p["ffn1_wd"].astype(BF16),
        "ffn2_wg": p["ffn2_wg"].astype(BF16), "ffn2_wu": p["ffn2_wu"].astype(BF16),
        "ffn2_wd": p["ffn2_wd"].astype(BF16),
    }


def _state_to_lanes(s):
    b = s.shape[0]
    return jnp.transpose(s, (0, 2, 1, 3)).reshape(b, RW_HEAD, RW_DIM)


def _state_from_lanes(s):
    b = s.shape[0]
    return jnp.transpose(s.reshape(b, RW_HEAD, RW_HEADS, RW_HEAD), (0, 2, 1, 3))


def _layer(x, cos, sin, shift_in, s0, w, final_norm, attend):
    bsz, t, _ = x.shape
    n = bsz * t
    x1, h = _ffn1(x.reshape(n, D_MODEL), w["ffn1_norm"], w["ffn1_wg"], w["ffn1_wu"], w["ffn1_wd"], w["mix_norm"])
    q, kv, ckv, kpe, u_rw, gate = _proj(h, cos, sin, w, None)
    o_a = attend(q, kv)
    o_b, s_new = _rwkv(u_rw, shift_in.reshape(bsz, 1, RW_COLS), _state_to_lanes(s0), w, bsz, t)
    y = _merge(x1, o_a, o_b, gate, w["w_out"], w["ffn2_norm"], w["ffn2_wg"], w["ffn2_wu"], w["ffn2_wd"],
               final_norm.reshape(1, D_MODEL))
    return (y.reshape(bsz, t, D_MODEL), ckv.reshape(bsz, t, KV_LORA), kpe.reshape(bsz, t, QK_ROPE),
            _state_from_lanes(s_new), u_rw.reshape(bsz, t, RW_COLS)[:, -1])


def kernel(x_prompt, x_sample, cache_ckv, cache_kpe, state_wkv, state_shift, page_table, ffn1_norm, ffn1_wg,
           ffn1_wu, ffn1_wd, mix_norm, w_in, q_norm, kv_norm, w_qb, w_uk, w_uv, w_o_mla, rw_mu, rw_w0, rw_w2,
           rw_a0, rw_a2, rw_g2, rw_kk, rw_ka, rw_rk, rw_ln_w, rw_ln_b, w_o_rw, w_out, ffn2_norm, ffn2_wg,
           ffn2_wu, ffn2_wd, final_norm):
    depth = w_in.shape[0]
    assert depth == 1, "the final norm is fused into the layer's last stage"
    layer_params = dict(ffn1_norm=ffn1_norm, ffn1_wg=ffn1_wg, ffn1_wu=ffn1_wu, ffn1_wd=ffn1_wd, mix_norm=mix_norm,
                        w_in=w_in, q_norm=q_norm, kv_norm=kv_norm, w_qb=w_qb, w_uk=w_uk, w_uv=w_uv,
                        w_o_mla=w_o_mla, rw_mu=rw_mu, rw_w0=rw_w0, rw_w2=rw_w2, rw_a0=rw_a0, rw_a2=rw_a2,
                        rw_g2=rw_g2, rw_kk=rw_kk, rw_ka=rw_ka, rw_rk=rw_rk, rw_ln_w=rw_ln_w, rw_ln_b=rw_ln_b,
                        w_o_rw=w_o_rw, w_out=w_out, ffn2_norm=ffn2_norm, ffn2_wg=ffn2_wg, ffn2_wu=ffn2_wu,
                        ffn2_wd=ffn2_wd)
    w = _prep_weights({k: v[0] for k, v in layer_params.items()})

    bp, tp, _ = x_prompt.shape
    bs, ts, _ = x_sample.shape
    n_pages = page_table.shape[1]
    past_len = n_pages * cache_ckv.shape[2]

    cos_p, sin_p = _rope_tables(jnp.arange(tp, dtype=jnp.int32))
    attend_p = lambda q, kv: _attn_prompt(q, kv, w["w_uv"], w["w_o_mla"], bp, tp)
    out_p = _layer(x_prompt, cos_p, sin_p, jnp.zeros((bp, RW_COLS), F32),
                   jnp.zeros((bp, RW_HEADS, RW_HEAD, RW_HEAD), F32), w, final_norm, attend_p)

    n_s = bs * ts
    tile_s = _pick_tile(n_s, 512)
    pos_s = past_len + jnp.arange(tile_s, dtype=jnp.int32) % ts
    cos_s, sin_s = _rope_tables(pos_s)

    def attend_s(q, kv):
        q_b = jnp.transpose(q.reshape(MLA_HEADS, bs, ts, QK_WIDTH), (1, 0, 2, 3)).reshape(bs, MLA_HEADS * ts, QK_WIDTH)
        kv_new = kv.astype(F32).reshape(bs, ts, QK_WIDTH)
        o_lat = _attn_sample(page_table, q_b, kv_new, cache_ckv[0], cache_kpe[0], _pick_tile(n_pages, 8))
        o_lat = jnp.transpose(o_lat.reshape(bs, MLA_HEADS, ts, KV_LORA), (1, 0, 2, 3)).reshape(MLA_HEADS, n_s, KV_LORA)
        return _mla_out_call(o_lat, w["w_uv"], w["w_o_mla"])

    out_s = _layer(x_sample, cos_s, sin_s, state_shift[0], state_wkv[0], w, final_norm, attend_s)

    y_p, ckv_p, kpe_p, wkv_p, sh_p = out_p
    y_s, ckv_s, kpe_s, wkv_s, sh_s = out_s
    stack = lambda z: z[None]
    return (y_p, y_s, stack(ckv_p), stack(kpe_p), stack(wkv_p), stack(sh_p),
            stack(ckv_s), stack(kpe_s), stack(wkv_s), stack(sh_s))
```

```python
import functools

import jax
import jax.numpy as jnp
from jax import lax
from jax.experimental import pallas as pl
from jax.experimental.pallas import tpu as pltpu

F32 = jnp.float32
BF16 = jnp.bfloat16

D_MODEL = 1024
MLA_HEADS = 8
QK_NOPE = 64
QK_ROPE = 32
V_DIM = 64
Q_LORA = 384
KV_LORA = 256
ROPE_THETA = 10000.0
SM_SCALE = (QK_NOPE + QK_ROPE) ** -0.5
RW_HEADS = 8
RW_HEAD = 64
RW_DIM = RW_HEADS * RW_HEAD
DECAY_LORA = 64
AAA_LORA = 64
GATE_LORA = 128
GN_EPS = RW_HEAD * 1e-5
D_FF = 2816
RMS_EPS = 1e-6
NEG_INF = -1e30
MLA_COLS = Q_LORA + KV_LORA + QK_ROPE
RW_COLS = 3 * RW_DIM + DECAY_LORA + AAA_LORA + GATE_LORA

LANES = 128
ROPE_PAD = LANES
QK_WIDTH = KV_LORA + ROPE_PAD
FF_CHUNKS = 2
VMEM_LIMIT = 56 * 1024 * 1024

_NT = (((1,), (1,)), ((), ()))
_TN = (((0,), (0,)), ((), ()))


def _pick_tile(n, pref):
    t = pref
    while n % t:
        t //= 2
    return t


def _const_spec(shape):
    nd = len(shape)
    return pl.BlockSpec(shape, lambda *_: (0,) * nd, pipeline_mode=pl.Buffered(1))


def _params(sem):
    return pltpu.CompilerParams(dimension_semantics=sem, vmem_limit_bytes=VMEM_LIMIT)


def _dot(a, b):
    return jnp.dot(a, b, preferred_element_type=F32)


def _dot_nt(a, b):
    return lax.dot_general(a, b, _NT, preferred_element_type=F32)


def _dot_tn(a, b):
    return lax.dot_general(a, b, _TN, preferred_element_type=F32)


def _rmsnorm(x, g):
    return x * lax.rsqrt(jnp.mean(x * x, axis=-1, keepdims=True) + RMS_EPS) * g


def _swiglu(hn, wg_ref, wu_ref, wd_ref):
    fc = D_FF // FF_CHUNKS
    acc = None
    for c in range(FF_CHUNKS):
        g = _dot(hn, wg_ref[:, c * fc:(c + 1) * fc])
        u = _dot(hn, wu_ref[:, c * fc:(c + 1) * fc])
        act = (g * jax.nn.sigmoid(g) * u).astype(BF16)
        part = _dot(act, wd_ref[c * fc:(c + 1) * fc, :])
        acc = part if acc is None else acc + part
    return acc


def _ffn1_kernel(x_ref, n1_ref, wg_ref, wu_ref, wd_ref, nm_ref, x1_ref, h_ref):
    x = x_ref[...]
    hn = _rmsnorm(x, n1_ref[...]).astype(BF16)
    x1 = x + 0.5 * _swiglu(hn, wg_ref, wu_ref, wd_ref)
    x1_ref[...] = x1
    h_ref[...] = _rmsnorm(x1, nm_ref[...]).astype(BF16)


def _ffn1(x, n1, wg, wu, wd, nm):
    n = x.shape[0]
    tm = _pick_tile(n, 512)
    row = lambda w: pl.BlockSpec((tm, w), lambda i: (i, 0))
    return pl.pallas_call(
        _ffn1_kernel,
        grid=(n // tm,),
        in_specs=[row(D_MODEL), _const_spec((1, D_MODEL)), _const_spec((D_MODEL, D_FF)),
                  _const_spec((D_MODEL, D_FF)), _const_spec((D_FF, D_MODEL)), _const_spec((1, D_MODEL))],
        out_specs=[row(D_MODEL), row(D_MODEL)],
        out_shape=[jax.ShapeDtypeStruct((n, D_MODEL), F32), jax.ShapeDtypeStruct((n, D_MODEL), BF16)],
        compiler_params=_params(("parallel",)),
        name="ffn1",
    )(x, n1, wg, wu, wd, nm)


def _proj_kernel(h_ref, cos_ref, sin_ref, wmla_ref, wrw_ref, wgate_ref, qn_ref, kvn_ref,
                 wnope_ref, wpe_ref, wper_ref, wuk_ref,
                 q_ref, kv_ref, ckv_ref, kpe_ref, urw_ref, gate_ref):
    h = h_ref[...]
    cos = cos_ref[...]
    sin = sin_ref[...]
    u = _dot(h, wmla_ref[...])
    qa = u[:, :Q_LORA]
    c0 = Q_LORA + KV_LORA
    ckv = _rmsnorm(u[:, Q_LORA:c0], kvn_ref[...])
    kpe = u[:, c0:c0 + ROPE_PAD] * cos + u[:, c0 + ROPE_PAD:] * sin
    ckv_ref[...] = ckv
    kpe_ref[...] = kpe[:, :QK_ROPE]
    kv_ref[...] = jnp.concatenate([ckv, kpe], axis=1).astype(BF16)

    qn = _rmsnorm(qa, qn_ref[...]).astype(BF16)
    q_nope = _dot(qn, wnope_ref[...]).astype(BF16)
    q_lat = _dot(q_nope, wuk_ref[...])
    qp = _dot(qn, wpe_ref[...])
    qr = _dot(qn, wper_ref[...])
    for hh in range(MLA_HEADS):
        ql = q_lat[:, hh * KV_LORA:(hh + 1) * KV_LORA] * SM_SCALE
        sl = slice(hh * ROPE_PAD, (hh + 1) * ROPE_PAD)
        pe = (qp[:, sl] * cos + qr[:, sl] * sin) * SM_SCALE
        q_ref[hh] = jnp.concatenate([ql, pe], axis=1).astype(BF16)

    urw_ref[...] = _dot(h, wrw_ref[...])
    gate_ref[...] = jax.nn.sigmoid(_dot(h, wgate_ref[...]))


def _proj(h, cos, sin, w):
    n = h.shape[0]
    tm = _pick_tile(cos.shape[0], _pick_tile(n, 512))
    nrb = cos.shape[0] // tm
    row = lambda wd_: pl.BlockSpec((tm, wd_), lambda i: (i, 0))
    rope = pl.BlockSpec((tm, ROPE_PAD), lambda i: (i % nrb, 0))
    wmla_w = Q_LORA + KV_LORA + 2 * ROPE_PAD
    return pl.pallas_call(
        _proj_kernel,
        grid=(n // tm,),
        in_specs=[row(D_MODEL), rope, rope,
                  _const_spec((D_MODEL, wmla_w)), _const_spec((D_MODEL, RW_COLS)),
                  _const_spec((D_MODEL, 2 * D_MODEL)), _const_spec((1, Q_LORA)), _const_spec((1, KV_LORA)),
                  _const_spec((Q_LORA, MLA_HEADS * QK_NOPE)), _const_spec((Q_LORA, MLA_HEADS * ROPE_PAD)),
                  _const_spec((Q_LORA, MLA_HEADS * ROPE_PAD)),
                  _const_spec((MLA_HEADS * QK_NOPE, MLA_HEADS * KV_LORA))],
        out_specs=[pl.BlockSpec((MLA_HEADS, tm, QK_WIDTH), lambda i: (0, i, 0)),
                   row(QK_WIDTH), row(KV_LORA), row(QK_ROPE), row(RW_COLS), row(2 * D_MODEL)],
        out_shape=[jax.ShapeDtypeStruct((MLA_HEADS, n, QK_WIDTH), BF16),
                   jax.ShapeDtypeStruct((n, QK_WIDTH), BF16),
                   jax.ShapeDtypeStruct((n, KV_LORA), F32),
                   jax.ShapeDtypeStruct((n, QK_ROPE), F32),
                   jax.ShapeDtypeStruct((n, RW_COLS), F32),
                   jax.ShapeDtypeStruct((n, 2 * D_MODEL), F32)],
        compiler_params=_params(("parallel",)),
        name="proj",
    )(h, cos, sin, w["w_mla"], w["w_rw"], w["w_gate"], w["q_norm"], w["kv_norm"],
      w["w_nope"], w["w_pe"], w["w_pe_rot"], w["w_uk_bd"])


def _lane_tile(x, n):
    if n <= LANES:
        return x[:, :n]
    return jnp.concatenate([x] * (n // LANES), axis=1)


def _softmax_update(s, v, m_ref, l_ref, acc_ref, rows):
    m_prev = m_ref[rows, :]
    m_new = jnp.maximum(m_prev, jnp.max(s, axis=-1, keepdims=True))
    alpha = jnp.exp(m_prev - m_new)
    p = jnp.exp(s - _lane_tile(m_new, s.shape[1]))
    l_ref[rows, :] = alpha * l_ref[rows, :] + jnp.sum(p, axis=-1, keepdims=True)
    acc_ref[rows, :] = _lane_tile(alpha, KV_LORA) * acc_ref[rows, :] + _dot(p.astype(BF16), v)
    m_ref[rows, :] = m_new


def _softmax_init(m_ref, l_ref, acc_ref):
    m_ref[...] = jnp.full(m_ref.shape, NEG_INF, F32)
    l_ref[...] = jnp.zeros(l_ref.shape, F32)
    acc_ref[...] = jnp.zeros(acc_ref.shape, F32)


def _softmax_result(l_ref, acc_ref):
    return acc_ref[...] / _lane_tile(l_ref[...], KV_LORA)


def _mla_out(o_lat, wuv_ref, wo_ref, rows):
    heads = [_dot(o_lat[hh * rows:(hh + 1) * rows].astype(BF16), wuv_ref[hh]).astype(BF16)
             for hh in range(MLA_HEADS)]
    return _dot(jnp.concatenate(heads, axis=1), wo_ref[...])


ATTN_SPLIT = 2


def _attn_prompt_kernel(q_ref, kv_ref, wuv_ref, wo_ref, o_ref, m_ref, l_ref, acc_ref, *, tq, tk):
    i = pl.program_id(1)
    hs = MLA_HEADS // ATTN_SPLIT
    sub = hs * tq
    _softmax_init(m_ref, l_ref, acc_ref)
    n_full = (i * tq) // tk

    def chunks(starts, masked):
        kvs = [kv_ref[pl.ds(k0, tk), :] for k0 in starts]
        if masked:
            delta = (lax.broadcasted_iota(jnp.int32, (sub, tk), 1)
                     - lax.broadcasted_iota(jnp.int32, (sub, tk), 0) % tq)
        for g in range(ATTN_SPLIT):
            q = q_ref[g * hs:(g + 1) * hs].reshape(sub, QK_WIDTH)
            scores = [_dot_nt(q, kv) for kv in kvs]
            for k0, kv, s in zip(starts, kvs, scores):
                if masked:
                    s = jnp.where(delta <= i * tq - k0, s, NEG_INF)
                _softmax_update(s, kv[:, :KV_LORA], m_ref, l_ref, acc_ref, pl.ds(g * sub, sub))

    def pair_step(j, carry):
        k0 = pl.multiple_of(2 * j * tk, 2 * tk)
        chunks([k0, pl.multiple_of(k0 + tk, tk)], False)
        return carry

    lax.fori_loop(0, n_full // 2, pair_step, 0)

    @pl.when(n_full % 2 == 1)
    def _():
        chunks([pl.multiple_of((n_full - 1) * tk, tk)], False)

    chunks([pl.multiple_of(n_full * tk, tk)], True)
    o_ref[...] = _mla_out(_softmax_result(l_ref, acc_ref), wuv_ref, wo_ref, tq)


def _attn_prompt(q, kv, wuv, wo, bsz, t):
    tq = _pick_tile(t, 128)
    tk = _pick_tile(t, 512)
    nq = t // tq
    rows = MLA_HEADS * tq
    kern = functools.partial(_attn_prompt_kernel, tq=tq, tk=tk)
    return pl.pallas_call(
        kern,
        grid=(bsz, nq),
        in_specs=[pl.BlockSpec((MLA_HEADS, tq, QK_WIDTH), lambda b, i: (0, b * nq + i, 0)),
                  pl.BlockSpec((t, QK_WIDTH), lambda b, i: (b, 0)),
                  _const_spec((MLA_HEADS, KV_LORA, V_DIM)), _const_spec((MLA_HEADS * V_DIM, D_MODEL))],
        out_specs=pl.BlockSpec((tq, D_MODEL), lambda b, i: (b * nq + i, 0)),
        out_shape=jax.ShapeDtypeStruct((bsz * t, D_MODEL), F32),
        scratch_shapes=[pltpu.VMEM((rows, LANES), F32), pltpu.VMEM((rows, LANES), F32),
                        pltpu.VMEM((rows, KV_LORA), F32)],
        compiler_params=_params(("parallel", "arbitrary")),
        name="attn_prompt",
    )(q, kv, wuv, wo)


SAMPLE_STREAMS = 1

def _attn_sample_kernel(pt_ref, q_ref, kvn_ref, cc_hbm, ck_hbm, o_ref,
                        cbuf, kbuf, csem, ksem, m_ref, l_ref, acc_ref, *, pgc, n_pages, tnew):
    b = pl.program_id(0)
    nch = n_pages // pgc
    page_rows = cbuf.shape[2]

    def chunk_copies(seq, ch):
        slot = ch % 2
        out = []
        for p in range(pgc):
            page = pt_ref[seq * n_pages + ch * pgc + p]
            out.append(pltpu.make_async_copy(cc_hbm.at[page], cbuf.at[slot, p], csem.at[slot]))
            out.append(pltpu.make_async_copy(ck_hbm.at[page], kbuf.at[slot, p], ksem.at[slot]))
        return out

    @pl.when(b == 0)
    def _():
        for cp in chunk_copies(b, 0):
            cp.start()

    _softmax_init(m_ref, l_ref, acc_ref)
    q = q_ref[...]
    rows = q.shape[0]
    q_lat = q[:, :KV_LORA]
    q_pe = q[:, KV_LORA:KV_LORA + QK_ROPE]
    part = pgc // SAMPLE_STREAMS

    for ch in range(nch):
        if ch + 1 < nch:
            for cp in chunk_copies(b, ch + 1):
                cp.start()
        else:
            @pl.when(b + 1 < pl.num_programs(0))
            def _():
                for cp in chunk_copies(b + 1, 0):
                    cp.start()
        for cp in chunk_copies(b, ch):
            cp.wait()
        slot = ch % 2
        for st in range(SAMPLE_STREAMS):
            c = cbuf[slot, st * part:(st + 1) * part].reshape(part * page_rows, KV_LORA).astype(BF16)
            kt = jnp.concatenate([kbuf[slot, p] for p in range(st * part, (st + 1) * part)],
                                 axis=1).astype(BF16)
            s = _dot_nt(q_lat, c) + _dot(q_pe, kt)
            _softmax_update(s, c, m_ref, l_ref, acc_ref, pl.ds(st * rows, rows))

    kvn = kvn_ref[...].astype(BF16)
    sn = _dot_nt(q, kvn)
    qt = lax.broadcasted_iota(jnp.int32, (rows, tnew), 0) % tnew
    kt_ = lax.broadcasted_iota(jnp.int32, (rows, tnew), 1)
    sn = jnp.where(kt_ <= qt, sn, NEG_INF)
    _softmax_update(sn, kvn[:, :KV_LORA], m_ref, l_ref, acc_ref, pl.ds(0, rows))

    m_parts = [m_ref[st * rows:(st + 1) * rows, :] for st in range(SAMPLE_STREAMS)]
    m_all = functools.reduce(jnp.maximum, m_parts)
    l_all = None
    acc_all = None
    for st in range(SAMPLE_STREAMS):
        scale = jnp.exp(m_parts[st] - m_all)
        l_st = scale * l_ref[st * rows:(st + 1) * rows, :]
        acc_st = _lane_tile(scale, KV_LORA) * acc_ref[st * rows:(st + 1) * rows, :]
        l_all = l_st if l_all is None else l_all + l_st
        acc_all = acc_st if acc_all is None else acc_all + acc_st
    o_ref[...] = acc_all / _lane_tile(l_all, KV_LORA)


def _attn_sample(page_table, q, kvn, cache_c, cache_kt):
    nb, n_pages = page_table.shape
    tnew = kvn.shape[1]
    rows = q.shape[1]
    page = cache_c.shape[1]
    pgc = _pick_tile(n_pages, 16)
    assert (n_pages // pgc) % 2 == 0, "two ring slots alternate statically, also across sequences"
    kern = functools.partial(_attn_sample_kernel, pgc=pgc, n_pages=n_pages, tnew=tnew)
    grid_spec = pltpu.PrefetchScalarGridSpec(
        num_scalar_prefetch=1,
        grid=(nb,),
        in_specs=[pl.BlockSpec((None, rows, QK_WIDTH), lambda b, pt: (b, 0, 0)),
                  pl.BlockSpec((None, tnew, QK_WIDTH), lambda b, pt: (b, 0, 0)),
                  pl.BlockSpec(memory_space=pl.ANY), pl.BlockSpec(memory_space=pl.ANY)],
        out_specs=pl.BlockSpec((None, rows, KV_LORA), lambda b, pt: (b, 0, 0)),
        scratch_shapes=[pltpu.VMEM((2, pgc, page, KV_LORA), F32), pltpu.VMEM((2, pgc, QK_ROPE, page), F32),
                        pltpu.SemaphoreType.DMA((2,)), pltpu.SemaphoreType.DMA((2,)),
                        pltpu.VMEM((SAMPLE_STREAMS * rows, LANES), F32),
                        pltpu.VMEM((SAMPLE_STREAMS * rows, LANES), F32),
                        pltpu.VMEM((SAMPLE_STREAMS * rows, KV_LORA), F32)],
    )
    return pl.pallas_call(
        kern,
        grid_spec=grid_spec,
        out_shape=jax.ShapeDtypeStruct((nb, rows, KV_LORA), F32),
        compiler_params=_params(("arbitrary",)),
        name="attn_sample",
    )(page_table.reshape(-1), q, kvn, cache_c, cache_kt)


def _mla_out_kernel(ol_ref, wuv_ref, wo_ref, o_ref):
    tm = ol_ref.shape[1]
    o_ref[...] = _mla_out(ol_ref[...].reshape(MLA_HEADS * tm, KV_LORA), wuv_ref, wo_ref, tm)


def _mla_out_call(o_lat, wuv, wo):
    n = o_lat.shape[1]
    tm = _pick_tile(n, 256)
    return pl.pallas_call(
        _mla_out_kernel,
        grid=(n // tm,),
        in_specs=[pl.BlockSpec((MLA_HEADS, tm, KV_LORA), lambda i: (0, i, 0)),
                  _const_spec((MLA_HEADS, KV_LORA, V_DIM)), _const_spec((MLA_HEADS * V_DIM, D_MODEL))],
        out_specs=pl.BlockSpec((tm, D_MODEL), lambda i: (i, 0)),
        out_shape=jax.ShapeDtypeStruct((n, D_MODEL), F32),
        compiler_params=_params(("parallel",)),
        name="mla_out",
    )(o_lat, wuv, wo)


def _split_dot(x, w_bf, pieces):
    out = None
    rem = x
    for _ in range(pieces):
        hi = rem.astype(BF16)
        part = _dot(hi, w_bf)
        out = part if out is None else out + part
        rem = rem - hi.astype(F32)
    return out


def _split_dot_lhs(tri_bf, x):
    out = None
    rem = x
    for _ in range(3):
        hi = rem.astype(BF16)
        part = _dot(tri_bf, hi)
        out = part if out is None else out + part
        rem = rem - hi.astype(F32)
    return out


def _expand_heads(x, hg):
    lane_head = lax.broadcasted_iota(jnp.int32, x.shape, 1) // RW_HEAD
    zero = jnp.zeros((), x.dtype)
    return jnp.concatenate([jnp.where(lane_head == hh, x, zero) for hh in range(hg)], axis=0)


def _fold_heads(x, hg):
    c = x.shape[0] // hg
    out = x[0:c]
    for hh in range(1, hg):
        out = out + x[hh * c:(hh + 1) * c]
    return out


def _unit_lower_inverse(m, c):
    n = m.shape[0]
    eye = (lax.broadcasted_iota(jnp.int32, (n, n), 0) == lax.broadcasted_iota(jnp.int32, (n, n), 1)).astype(F32)
    p = eye + m
    mk = m
    k = 2
    while k < c:
        mkb = mk.astype(BF16)
        mk = _dot(mkb, mkb)
        p = p + _dot(p.astype(BF16), mk.astype(BF16))
        k *= 2
    return p


def _rwkv_kernel(u_ref, sh_ref, s0_ref, mu_ref, w0_ref, w2a2_ref, a0_ref, g2_ref, kkw_ref, ka_ref, rk_ref,
                 lnw_ref, lnb_ref, ones_ref, wo_ref,
                 ob_ref, sout_ref,
                 state_ref, prev_ref, *, c, hg, nseq):
    t = pl.program_id(1)
    tt = u_ref.shape[0] // nseq
    nc = tt // c
    ng = RW_HEADS // hg
    w = hg * RW_HEAD
    r_ = hg * c

    blockmask = (lax.broadcasted_iota(jnp.int32, (w, w), 0) // RW_HEAD
                 == lax.broadcasted_iota(jnp.int32, (w, w), 1) // RW_HEAD)

    @pl.when(t == 0)
    def _():
        prev_ref[...] = sh_ref[...]
        for s in range(nseq):
            for g in range(ng):
                s0g = s0_ref[s, :, g * w:(g + 1) * w]
                state_ref[s, g] = jnp.where(blockmask, jnp.concatenate([s0g] * hg, axis=0), 0.0)

    u = u_ref[...]
    row = lax.broadcasted_iota(jnp.int32, u.shape, 0)
    u_prev = pltpu.roll(u, 1, 0)
    for s in range(nseq):
        u_prev = jnp.where(row == s * tt, prev_ref[s], u_prev)
        prev_ref[s] = u[(s + 1) * tt - 1:(s + 1) * tt, :]
    us = u + (u_prev - u) * mu_ref[...]
    r = us[:, 0:RW_DIM]
    k = us[:, RW_DIM:2 * RW_DIM]
    v = us[:, 2 * RW_DIM:3 * RW_DIM]
    o1 = 3 * RW_DIM
    wa = us[:, o1:o1 + DECAY_LORA + AAA_LORA]
    gd = us[:, o1 + DECAY_LORA + AAA_LORA:]
    lane = lax.broadcasted_iota(jnp.int32, wa.shape, 1)
    wa = jnp.where(lane < DECAY_LORA, jnp.tanh(wa), wa)
    lo = _dot(wa.astype(BF16), w2a2_ref[...])
    w_raw = -jax.nn.softplus(-(w0_ref[...] + lo[:, :RW_DIM])) - 0.5
    logw = -jnp.exp(w_raw)
    a = jax.nn.sigmoid(a0_ref[...] + lo[:, RW_DIM:])
    g = _dot(jax.nn.sigmoid(gd).astype(BF16), g2_ref[...])
    ones_bd = ones_ref[...]
    kk = k * kkw_ref[...]
    kk = kk / jnp.maximum(jnp.sqrt(_split_dot(kk * kk, ones_bd, 2)), 1e-12)
    k2 = k * (1.0 + (a - 1.0) * ka_ref[...])

    tri = (lax.broadcasted_iota(jnp.int32, (c, c), 1) <= lax.broadcasted_iota(jnp.int32, (c, c), 0)).astype(BF16)
    rr = lax.broadcasted_iota(jnp.int32, (r_, r_), 0)
    cc = lax.broadcasted_iota(jnp.int32, (r_, r_), 1)
    strict_bd = (rr // c == cc // c) & (cc % c < rr % c)
    incl_wide = (lax.broadcasted_iota(jnp.int32, (c, r_), 1) % c
                 <= lax.broadcasted_iota(jnp.int32, (c, r_), 0))

    pre = {}
    for s in range(nseq):
        for ci in range(nc):
            rows = slice(s * tt + ci * c, s * tt + (ci + 1) * c)
            lw = logw[rows]
            cs = _split_dot_lhs(tri, lw)
            e_cur = jnp.exp(cs)
            e_inv = jnp.exp(-cs)
            kkc = kk[rows]
            at = (-kkc * jnp.exp(cs - lw)).astype(BF16)
            bt = (kkc * a[rows] * e_inv).astype(BF16)
            kt = (k2[rows] * e_inv).astype(BF16)
            rt = (r[rows] * e_cur).astype(BF16)
            vb = v[rows].astype(BF16)
            for g_ in range(ng):
                sl = slice(g_ * w, (g_ + 1) * w)
                a_, b_, k_, r__, v_ = at[:, sl], bt[:, sl], kt[:, sl], rt[:, sl], vb[:, sl]
                ax, bx, kx, vx = (_expand_heads(z, hg) for z in (a_, b_, k_, v_))
                n1 = _dot_nt(jnp.concatenate([ax, r__], axis=0), jnp.concatenate([bx, kx], axis=0))
                m_ab = jnp.where(strict_bd, n1[:r_, :r_], 0.0)
                m_ak = jnp.where(strict_bd, n1[:r_, r_:], 0.0).astype(BF16)
                a_rb = jnp.where(incl_wide, n1[r_:, :r_], 0.0).astype(BF16)
                a_rk = jnp.where(incl_wide, n1[r_:, r_:], 0.0).astype(BF16)
                pre[s, ci, g_] = dict(
                    tinv=_unit_lower_inverse(m_ab, c).astype(BF16),
                    wv=_dot(m_ak, vx),
                    yv=_dot(a_rk, vx),
                    dsv=_dot_tn(v_, k_),
                    ar=jnp.concatenate([a_, r__], axis=0), a_rb=a_rb, b=b_, p_end=e_cur[c - 1:c, sl])

    y_rows = []
    for s in range(nseq):
        for ci in range(nc):
            ys = []
            for g_ in range(ng):
                z = pre[s, ci, g_]
                s_f32 = state_ref[s, g_]
                sa = _dot_nt(z["ar"], s_f32.astype(BF16))
                wx = _expand_heads(sa[:c], hg) + z["wv"]
                ux = _dot(z["tinv"], wx.astype(BF16))
                u_ = _fold_heads(ux, hg)
                ys.append(sa[c:] + _dot(z["a_rb"], ux.astype(BF16)) + z["yv"])
                ds = _dot_tn(u_.astype(BF16), z["b"]) + z["dsv"]
                state_ref[s, g_] = (s_f32 + jnp.where(blockmask, ds, 0.0)) * z["p_end"]
            y_rows.append(ys[0] if ng == 1 else jnp.concatenate(ys, axis=1))
    y = y_rows[0] if len(y_rows) == 1 else jnp.concatenate(y_rows, axis=0)

    inv_n = 1.0 / RW_HEAD
    mean = _split_dot(y, ones_bd, 2) * inv_n
    d = y - mean
    var = _split_dot(d * d, ones_bd, 2) * inv_n
    yn = d * lax.rsqrt(var + GN_EPS) * lnw_ref[...] + lnb_ref[...]
    bonus = _split_dot(r * k2 * rk_ref[...], ones_bd, 2) * v
    ob_ref[...] = _dot(((yn + bonus) * g).astype(BF16), wo_ref[...])

    @pl.when(t == pl.num_programs(1) - 1)
    def _():
        for s in range(nseq):
            for g_ in range(ng):
                sout_ref[s, :, g_ * w:(g_ + 1) * w] = _fold_heads(state_ref[s, g_], hg)


def _rwkv(u_rw, shift_in, s0, w, bsz, t):
    if t >= 64:
        c, hg, nseq = 64, 4, 1
        tt = _pick_tile(t, 256)
    else:
        c, hg, nseq = t, RW_HEADS, _pick_tile(bsz, 8)
        tt = t
    nt = t // tt
    ng = RW_HEADS // hg
    wd_ = hg * RW_HEAD
    kern = functools.partial(_rwkv_kernel, c=c, hg=hg, nseq=nseq)
    vec = lambda n: _const_spec((1, n))
    return pl.pallas_call(
        kern,
        grid=(bsz // nseq, nt),
        in_specs=[pl.BlockSpec((nseq * tt, RW_COLS), lambda b, i: (b * nt + i, 0)),
                  pl.BlockSpec((nseq, 1, RW_COLS), lambda b, i: (b, 0, 0)),
                  pl.BlockSpec((nseq, RW_HEAD, RW_DIM), lambda b, i: (b, 0, 0)),
                  vec(RW_COLS), vec(RW_DIM), _const_spec((DECAY_LORA + AAA_LORA, 2 * RW_DIM)), vec(RW_DIM),
                  _const_spec((GATE_LORA, RW_DIM)), vec(RW_DIM), vec(RW_DIM), vec(RW_DIM), vec(RW_DIM),
                  vec(RW_DIM), _const_spec((RW_DIM, RW_DIM)), _const_spec((RW_DIM, D_MODEL))],
        out_specs=[pl.BlockSpec((nseq * tt, D_MODEL), lambda b, i: (b * nt + i, 0)),
                   pl.BlockSpec((nseq, RW_HEAD, RW_DIM), lambda b, i: (b, 0, 0))],
        out_shape=[jax.ShapeDtypeStruct((bsz * t, D_MODEL), F32),
                   jax.ShapeDtypeStruct((bsz, RW_HEAD, RW_DIM), F32)],
        scratch_shapes=[pltpu.VMEM((nseq, ng, wd_, wd_), F32), pltpu.VMEM((nseq, 1, RW_COLS), F32)],
        compiler_params=_params(("parallel", "arbitrary")),
        name="rwkv",
    )(u_rw, shift_in, s0, w["rw_mu"], w["rw_w0"], w["rw_w2a2"], w["rw_a0"], w["rw_g2"], w["rw_kk"],
      w["rw_ka"], w["rw_rk"], w["rw_ln_w"], w["rw_ln_b"], w["ones_bd"], w["w_o_rw"])


def _merge_kernel(x1_ref, oa_ref, ob_ref, gate_ref, wout_ref, n2_ref, wg_ref, wu_ref, wd_ref, nf_ref, y_ref):
    mixed = (gate_ref[:, :D_MODEL] * oa_ref[...] + gate_ref[:, D_MODEL:] * ob_ref[...]).astype(BF16)
    x2 = x1_ref[...] + _dot(mixed, wout_ref[...])
    hn = _rmsnorm(x2, n2_ref[...]).astype(BF16)
    x3 = x2 + 0.5 * _swiglu(hn, wg_ref, wu_ref, wd_ref)
    y_ref[...] = _rmsnorm(x3, nf_ref[...])


def _merge(x1, oa, ob, gate, wout, n2, wg, wu, wd, nf):
    n = x1.shape[0]
    tm = _pick_tile(n, 256)
    row = lambda w_: pl.BlockSpec((tm, w_), lambda i: (i, 0))
    return pl.pallas_call(
        _merge_kernel,
        grid=(n // tm,),
        in_specs=[row(D_MODEL), row(D_MODEL), row(D_MODEL), row(2 * D_MODEL),
                  _const_spec((D_MODEL, D_MODEL)), _const_spec((1, D_MODEL)), _const_spec((D_MODEL, D_FF)),
                  _const_spec((D_MODEL, D_FF)), _const_spec((D_FF, D_MODEL)), _const_spec((1, D_MODEL))],
        out_specs=row(D_MODEL),
        out_shape=jax.ShapeDtypeStruct((n, D_MODEL), F32),
        compiler_params=_params(("parallel",)),
        name="merge_ffn2",
    )(x1, oa, ob, gate, wout, n2, wg, wu, wd, nf)


def _rope_tables(pos):
    inv = 1.0 / (ROPE_THETA ** (jnp.arange(0, QK_ROPE, 2, dtype=F32) / QK_ROPE))
    ang = pos.astype(F32)[:, None] * inv[None, :]
    pad = jnp.zeros((pos.shape[0], ROPE_PAD - QK_ROPE), F32)
    cos, sin = jnp.cos(ang), jnp.sin(ang)
    return jnp.concatenate([cos, cos, pad], axis=1), jnp.concatenate([sin, sin, pad], axis=1)


def _rot_cols(wm):
    half = QK_ROPE // 2
    return jnp.concatenate([-wm[..., half:], wm[..., :half]], axis=-1)


def _pad_cols(wm, width):
    return jnp.pad(wm, [(0, 0)] * (wm.ndim - 1) + [(0, width - wm.shape[-1])])


def _prep_weights(p):
    w_in = p["w_in"]
    w_kpe = w_in[:, Q_LORA + KV_LORA:MLA_COLS]
    w_mla = jnp.concatenate([w_in[:, :Q_LORA + KV_LORA], _pad_cols(w_kpe, ROPE_PAD),
                             _pad_cols(_rot_cols(w_kpe), ROPE_PAD)], axis=1)
    w_qb = p["w_qb"].reshape(Q_LORA, MLA_HEADS, QK_NOPE + QK_ROPE)
    w_nope = w_qb[:, :, :QK_NOPE].reshape(Q_LORA, MLA_HEADS * QK_NOPE)
    w_pe = w_qb[:, :, QK_NOPE:]
    w_uk = jnp.transpose(p["w_uk"], (1, 2, 0))
    eye = jnp.eye(MLA_HEADS, dtype=F32)
    w_uk_bd = (eye[:, None, :, None] * w_uk[:, :, None, :]).reshape(MLA_HEADS * QK_NOPE, MLA_HEADS * KV_LORA)
    zeros = jnp.zeros((DECAY_LORA, RW_DIM), F32)
    w2a2 = jnp.concatenate([jnp.concatenate([p["rw_w2"], zeros], axis=1),
                            jnp.concatenate([zeros, p["rw_a2"]], axis=1)], axis=0)
    head = jnp.arange(RW_DIM) // RW_HEAD
    row = lambda v: v.reshape(1, -1)
    return {
        "w_mla": w_mla.astype(BF16),
        "w_rw": w_in[:, MLA_COLS:MLA_COLS + RW_COLS].astype(BF16),
        "w_gate": w_in[:, MLA_COLS + RW_COLS:].astype(BF16),
        "q_norm": row(p["q_norm"]), "kv_norm": row(p["kv_norm"]),
        "w_nope": w_nope.astype(BF16),
        "w_pe": _pad_cols(w_pe, ROPE_PAD).reshape(Q_LORA, MLA_HEADS * ROPE_PAD).astype(BF16),
        "w_pe_rot": _pad_cols(_rot_cols(w_pe), ROPE_PAD).reshape(Q_LORA, MLA_HEADS * ROPE_PAD).astype(BF16),
        "w_uk_bd": w_uk_bd.astype(BF16),
        "w_uv": jnp.transpose(p["w_uv"], (1, 0, 2)).astype(BF16),
        "w_o_mla": p["w_o_mla"].astype(BF16),
        "rw_mu": row(p["rw_mu"]), "rw_w0": row(p["rw_w0"]), "rw_w2a2": w2a2.astype(BF16),
        "rw_a0": row(p["rw_a0"]), "rw_g2": p["rw_g2"].astype(BF16), "rw_kk": row(p["rw_kk"]),
        "rw_ka": row(p["rw_ka"]), "rw_rk": row(p["rw_rk"]), "rw_ln_w": row(p["rw_ln_w"]),
        "rw_ln_b": row(p["rw_ln_b"]),
        "ones_bd": (head[:, None] == head[None, :]).astype(BF16),
        "w_o_rw": p["w_o_rw"].astype(BF16),
        "w_out": p["w_out"].astype(BF16),
        "ffn1_norm": row(p["ffn1_norm"]), "mix_norm": row(p["mix_norm"]), "ffn2_norm": row(p["ffn2_norm"]),
        "ffn1_wg": p["ffn1_wg"].astype(BF16), "ffn1_wu": p["ffn1_wu"].astype(BF16),
        "ffn1_wd": p["ffn1_wd"].astype(BF16),
        "ffn2_wg": p["ffn2_wg"].astype(BF16), "ffn2_wu": p["ffn2_wu"].astype(BF16),
        "ffn2_wd": p["ffn2_wd"].astype(BF16),
    }


def _state_to_lanes(s):
    b = s.shape[0]
    return jnp.transpose(s, (0, 2, 1, 3)).reshape(b, RW_HEAD, RW_DIM)


def _state_from_lanes(s):
    b = s.shape[0]
    return jnp.transpose(s.reshape(b, RW_HEAD, RW_HEADS, RW_HEAD), (0, 2, 1, 3))


def _layer(x, cos, sin, shift_in, s0, w, final_norm, attend):
    bsz, t, _ = x.shape
    n = bsz * t
    x1, h = _ffn1(x.reshape(n, D_MODEL), w["ffn1_norm"], w["ffn1_wg"], w["ffn1_wu"], w["ffn1_wd"], w["mix_norm"])
    q, kv, ckv, kpe, u_rw, gate = _proj(h, cos, sin, w)
    o_a = attend(q, kv)
    o_b, s_new = _rwkv(u_rw, shift_in.reshape(bsz, 1, RW_COLS), _state_to_lanes(s0), w, bsz, t)
    y = _merge(x1, o_a, o_b, gate, w["w_out"], w["ffn2_norm"], w["ffn2_wg"], w["ffn2_wu"], w["ffn2_wd"],
               final_norm.reshape(1, D_MODEL))
    return (y.reshape(bsz, t, D_MODEL), ckv.reshape(bsz, t, KV_LORA), kpe.reshape(bsz, t, QK_ROPE),
            _state_from_lanes(s_new), u_rw.reshape(bsz, t, RW_COLS)[:, -1])


def kernel(x_prompt, x_sample, cache_ckv, cache_kpe, state_wkv, state_shift, page_table, ffn1_norm, ffn1_wg,
           ffn1_wu, ffn1_wd, mix_norm, w_in, q_norm, kv_norm, w_qb, w_uk, w_uv, w_o_mla, rw_mu, rw_w0, rw_w2,
           rw_a0, rw_a2, rw_g2, rw_kk, rw_ka, rw_rk, rw_ln_w, rw_ln_b, w_o_rw, w_out, ffn2_norm, ffn2_wg,
           ffn2_wu, ffn2_wd, final_norm):
    depth = w_in.shape[0]
    assert depth == 1, "the final norm is fused into the layer's last stage"
    layer_params = dict(ffn1_norm=ffn1_norm, ffn1_wg=ffn1_wg, ffn1_wu=ffn1_wu, ffn1_wd=ffn1_wd, mix_norm=mix_norm,
                        w_in=w_in, q_norm=q_norm, kv_norm=kv_norm, w_qb=w_qb, w_uk=w_uk, w_uv=w_uv,
                        w_o_mla=w_o_mla, rw_mu=rw_mu, rw_w0=rw_w0, rw_w2=rw_w2, rw_a0=rw_a0, rw_a2=rw_a2,
                        rw_g2=rw_g2, rw_kk=rw_kk, rw_ka=rw_ka, rw_rk=rw_rk, rw_ln_w=rw_ln_w, rw_ln_b=rw_ln_b,
                        w_o_rw=w_o_rw, w_out=w_out, ffn2_norm=ffn2_norm, ffn2_wg=ffn2_wg, ffn2_wu=ffn2_wu,
                        ffn2_wd=ffn2_wd)
    w = _prep_weights({k: v[0] for k, v in layer_params.items()})

    bp, tp, _ = x_prompt.shape
    bs, ts, _ = x_sample.shape
    n_pages = page_table.shape[1]
    past_len = n_pages * cache_ckv.shape[2]

    cos_p, sin_p = _rope_tables(jnp.arange(tp, dtype=jnp.int32))
    attend_p = lambda q, kv: _attn_prompt(q, kv, w["w_uv"], w["w_o_mla"], bp, tp)
    out_p = _layer(x_prompt, cos_p, sin_p, jnp.zeros((bp, RW_COLS), F32),
                   jnp.zeros((bp, RW_HEADS, RW_HEAD, RW_HEAD), F32), w, final_norm, attend_p)

    n_s = bs * ts
    tile_s = _pick_tile(n_s, 512)
    pos_s = past_len + jnp.arange(tile_s, dtype=jnp.int32) % ts
    cos_s, sin_s = _rope_tables(pos_s)

    def attend_s(q, kv):
        q_b = jnp.transpose(q.reshape(MLA_HEADS, bs, ts, QK_WIDTH), (1, 0, 2, 3)).reshape(bs, MLA_HEADS * ts, QK_WIDTH)
        kv_new = kv.astype(F32).reshape(bs, ts, QK_WIDTH)
        o_lat = _attn_sample(page_table, q_b, kv_new, cache_ckv[0], jnp.swapaxes(cache_kpe[0], 1, 2))
        o_lat = jnp.transpose(o_lat.reshape(bs, MLA_HEADS, ts, KV_LORA), (1, 0, 2, 3)).reshape(MLA_HEADS, n_s, KV_LORA)
        return _mla_out_call(o_lat, w["w_uv"], w["w_o_mla"])

    out_s = _layer(x_sample, cos_s, sin_s, state_shift[0], state_wkv[0], w, final_norm, attend_s)

    y_p, ckv_p, kpe_p, wkv_p, sh_p = out_p
    y_s, ckv_s, kpe_s, wkv_s, sh_s = out_s
    stack = lambda z: z[None]
    return (y_p, y_s, stack(ckv_p), stack(kpe_p), stack(wkv_p), stack(sh_p),
            stack(ckv_s), stack(kpe_s), stack(wkv_s), stack(sh_s))
```

```python
import functools

import jax
import jax.numpy as jnp
from jax import lax
from jax.experimental import pallas as pl
from jax.experimental.pallas import tpu as pltpu

F32 = jnp.float32
BF16 = jnp.bfloat16

D_MODEL = 1024
MLA_HEADS = 8
QK_NOPE = 64
QK_ROPE = 32
V_DIM = 64
Q_LORA = 384
KV_LORA = 256
ROPE_THETA = 10000.0
SM_SCALE = (QK_NOPE + QK_ROPE) ** -0.5
RW_HEADS = 8
RW_HEAD = 64
RW_DIM = RW_HEADS * RW_HEAD
DECAY_LORA = 64
AAA_LORA = 64
GATE_LORA = 128
GN_EPS = RW_HEAD * 1e-5
D_FF = 2816
RMS_EPS = 1e-6
NEG_INF = -1e30
MLA_COLS = Q_LORA + KV_LORA + QK_ROPE
RW_COLS = 3 * RW_DIM + DECAY_LORA + AAA_LORA + GATE_LORA

LANES = 128
ROPE_PAD = LANES
QK_WIDTH = KV_LORA + ROPE_PAD
FF_CHUNKS = 2
VMEM_LIMIT = 56 * 1024 * 1024

_NT = (((1,), (1,)), ((), ()))
_TN = (((0,), (0,)), ((), ()))


def _pick_tile(n, pref):
    t = pref
    while n % t:
        t //= 2
    return t


def _const_spec(shape):
    nd = len(shape)
    return pl.BlockSpec(shape, lambda *_: (0,) * nd, pipeline_mode=pl.Buffered(1))


def _params(sem):
    return pltpu.CompilerParams(dimension_semantics=sem, vmem_limit_bytes=VMEM_LIMIT)


def _dot(a, b):
    return jnp.dot(a, b, preferred_element_type=F32)


def _dot_nt(a, b):
    return lax.dot_general(a, b, _NT, preferred_element_type=F32)


def _dot_tn(a, b):
    return lax.dot_general(a, b, _TN, preferred_element_type=F32)


def _rmsnorm(x, g):
    return x * lax.rsqrt(jnp.mean(x * x, axis=-1, keepdims=True) + RMS_EPS) * g


def _swiglu(hn, wg_ref, wu_ref, wd_ref):
    fc = D_FF // FF_CHUNKS
    acc = None
    for c in range(FF_CHUNKS):
        g = _dot(hn, wg_ref[:, c * fc:(c + 1) * fc])
        u = _dot(hn, wu_ref[:, c * fc:(c + 1) * fc])
        act = (g * jax.nn.sigmoid(g) * u).astype(BF16)
        part = _dot(act, wd_ref[c * fc:(c + 1) * fc, :])
        acc = part if acc is None else acc + part
    return acc


def _ffn1_kernel(x_ref, n1_ref, wg_ref, wu_ref, wd_ref, nm_ref, x1_ref, h_ref):
    x = x_ref[...]
    hn = _rmsnorm(x, n1_ref[...]).astype(BF16)
    x1 = x + 0.5 * _swiglu(hn, wg_ref, wu_ref, wd_ref)
    x1_ref[...] = x1
    h_ref[...] = _rmsnorm(x1, nm_ref[...]).astype(BF16)


def _ffn1(x, n1, wg, wu, wd, nm):
    n = x.shape[0]
    tm = _pick_tile(n, 512)
    row = lambda w: pl.BlockSpec((tm, w), lambda i: (i, 0))
    return pl.pallas_call(
        _ffn1_kernel,
        grid=(n // tm,),
        in_specs=[row(D_MODEL), _const_spec((1, D_MODEL)), _const_spec((D_MODEL, D_FF)),
                  _const_spec((D_MODEL, D_FF)), _const_spec((D_FF, D_MODEL)), _const_spec((1, D_MODEL))],
        out_specs=[row(D_MODEL), row(D_MODEL)],
        out_shape=[jax.ShapeDtypeStruct((n, D_MODEL), F32), jax.ShapeDtypeStruct((n, D_MODEL), BF16)],
        compiler_params=_params(("parallel",)),
        name="ffn1",
    )(x, n1, wg, wu, wd, nm)


def _proj_kernel(h_ref, cos_ref, sin_ref, wmla_ref, wrw_ref, wgate_ref, qn_ref, kvn_ref,
                 wnope_ref, wpe_ref, wper_ref, wuk_ref,
                 q_ref, kv_ref, ckv_ref, kpe_ref, urw_ref, gate_ref):
    h = h_ref[...]
    cos = cos_ref[...]
    sin = sin_ref[...]
    u = _dot(h, wmla_ref[...])
    qa = u[:, :Q_LORA]
    c0 = Q_LORA + KV_LORA
    ckv = _rmsnorm(u[:, Q_LORA:c0], kvn_ref[...])
    kpe = u[:, c0:c0 + ROPE_PAD] * cos + u[:, c0 + ROPE_PAD:] * sin
    ckv_ref[...] = ckv
    kpe_ref[...] = kpe[:, :QK_ROPE]
    kv_ref[...] = jnp.concatenate([ckv, kpe], axis=1).astype(BF16)

    qn = _rmsnorm(qa, qn_ref[...]).astype(BF16)
    q_nope = _dot(qn, wnope_ref[...]).astype(BF16)
    q_lat = _dot(q_nope, wuk_ref[...])
    qp = _dot(qn, wpe_ref[...])
    qr = _dot(qn, wper_ref[...])
    for hh in range(MLA_HEADS):
        ql = q_lat[:, hh * KV_LORA:(hh + 1) * KV_LORA] * SM_SCALE
        sl = slice(hh * ROPE_PAD, (hh + 1) * ROPE_PAD)
        pe = (qp[:, sl] * cos + qr[:, sl] * sin) * SM_SCALE
        q_ref[hh] = jnp.concatenate([ql, pe], axis=1).astype(BF16)

    urw_ref[...] = _dot(h, wrw_ref[...])
    gate_ref[...] = jax.nn.sigmoid(_dot(h, wgate_ref[...]))


def _proj(h, cos, sin, w):
    n = h.shape[0]
    tm = _pick_tile(cos.shape[0], _pick_tile(n, 512))
    nrb = cos.shape[0] // tm
    row = lambda wd_: pl.BlockSpec((tm, wd_), lambda i: (i, 0))
    rope = pl.BlockSpec((tm, ROPE_PAD), lambda i: (i % nrb, 0))
    wmla_w = Q_LORA + KV_LORA + 2 * ROPE_PAD
    return pl.pallas_call(
        _proj_kernel,
        grid=(n // tm,),
        in_specs=[row(D_MODEL), rope, rope,
                  _const_spec((D_MODEL, wmla_w)), _const_spec((D_MODEL, RW_COLS)),
                  _const_spec((D_MODEL, 2 * D_MODEL)), _const_spec((1, Q_LORA)), _const_spec((1, KV_LORA)),
                  _const_spec((Q_LORA, MLA_HEADS * QK_NOPE)), _const_spec((Q_LORA, MLA_HEADS * ROPE_PAD)),
                  _const_spec((Q_LORA, MLA_HEADS * ROPE_PAD)),
                  _const_spec((MLA_HEADS * QK_NOPE, MLA_HEADS * KV_LORA))],
        out_specs=[pl.BlockSpec((MLA_HEADS, tm, QK_WIDTH), lambda i: (0, i, 0)),
                   row(QK_WIDTH), row(KV_LORA), row(QK_ROPE), row(RW_COLS), row(2 * D_MODEL)],
        out_shape=[jax.ShapeDtypeStruct((MLA_HEADS, n, QK_WIDTH), BF16),
                   jax.ShapeDtypeStruct((n, QK_WIDTH), BF16),
                   jax.ShapeDtypeStruct((n, KV_LORA), F32),
                   jax.ShapeDtypeStruct((n, QK_ROPE), F32),
                   jax.ShapeDtypeStruct((n, RW_COLS), F32),
                   jax.ShapeDtypeStruct((n, 2 * D_MODEL), F32)],
        compiler_params=_params(("parallel",)),
        name="proj",
    )(h, cos, sin, w["w_mla"], w["w_rw"], w["w_gate"], w["q_norm"], w["kv_norm"],
      w["w_nope"], w["w_pe"], w["w_pe_rot"], w["w_uk_bd"])


def _lane_tile(x, n):
    if n <= LANES:
        return x[:, :n]
    return jnp.concatenate([x] * (n // LANES), axis=1)


def _softmax_update(s, v, m_ref, l_ref, acc_ref, rows):
    m_prev = m_ref[rows, :]
    m_new = jnp.maximum(m_prev, jnp.max(s, axis=-1, keepdims=True))
    alpha = jnp.exp(m_prev - m_new)
    p = jnp.exp(s - _lane_tile(m_new, s.shape[1]))
    l_ref[rows, :] = alpha * l_ref[rows, :] + jnp.sum(p, axis=-1, keepdims=True)
    acc_ref[rows, :] = _lane_tile(alpha, KV_LORA) * acc_ref[rows, :] + _dot(p.astype(BF16), v)
    m_ref[rows, :] = m_new


def _softmax_init(m_ref, l_ref, acc_ref):
    m_ref[...] = jnp.full(m_ref.shape, NEG_INF, F32)
    l_ref[...] = jnp.zeros(l_ref.shape, F32)
    acc_ref[...] = jnp.zeros(acc_ref.shape, F32)


def _softmax_result(l_ref, acc_ref):
    return acc_ref[...] / _lane_tile(l_ref[...], KV_LORA)


def _mla_out(o_lat, wuv_ref, wo_ref, rows):
    heads = [_dot(o_lat[hh * rows:(hh + 1) * rows].astype(BF16), wuv_ref[hh]).astype(BF16)
             for hh in range(MLA_HEADS)]
    return _dot(jnp.concatenate(heads, axis=1), wo_ref[...])


ATTN_SPLIT = 2


def _attn_prompt_kernel(q_ref, kv_ref, wuv_ref, wo_ref, o_ref, m_ref, l_ref, acc_ref, *, tq, tk):
    i = pl.program_id(1)
    hs = MLA_HEADS // ATTN_SPLIT
    sub = hs * tq
    _softmax_init(m_ref, l_ref, acc_ref)
    n_full = (i * tq) // tk

    def chunks(starts, masked):
        kvs = [kv_ref[pl.ds(k0, tk), :] for k0 in starts]
        if masked:
            delta = (lax.broadcasted_iota(jnp.int32, (sub, tk), 1)
                     - lax.broadcasted_iota(jnp.int32, (sub, tk), 0) % tq)
        for g in range(ATTN_SPLIT):
            q = q_ref[g * hs:(g + 1) * hs].reshape(sub, QK_WIDTH)
            scores = [_dot_nt(q, kv) for kv in kvs]
            for k0, kv, s in zip(starts, kvs, scores):
                if masked:
                    s = jnp.where(delta <= i * tq - k0, s, NEG_INF)
                _softmax_update(s, kv[:, :KV_LORA], m_ref, l_ref, acc_ref, pl.ds(g * sub, sub))

    def pair_step(j, carry):
        k0 = pl.multiple_of(2 * j * tk, 2 * tk)
        chunks([k0, pl.multiple_of(k0 + tk, tk)], False)
        return carry

    lax.fori_loop(0, n_full // 2, pair_step, 0)

    @pl.when(n_full % 2 == 1)
    def _():
        chunks([pl.multiple_of((n_full - 1) * tk, tk)], False)

    chunks([pl.multiple_of(n_full * tk, tk)], True)
    o_ref[...] = _mla_out(_softmax_result(l_ref, acc_ref), wuv_ref, wo_ref, tq)


def _attn_prompt(q, kv, wuv, wo, bsz, t):
    tq = _pick_tile(t, 128)
    tk = _pick_tile(t, 512)
    nq = t // tq
    rows = MLA_HEADS * tq
    kern = functools.partial(_attn_prompt_kernel, tq=tq, tk=tk)
    return pl.pallas_call(
        kern,
        grid=(bsz, nq),
        in_specs=[pl.BlockSpec((MLA_HEADS, tq, QK_WIDTH), lambda b, i: (0, b * nq + i, 0)),
                  pl.BlockSpec((t, QK_WIDTH), lambda b, i: (b, 0)),
                  _const_spec((MLA_HEADS, KV_LORA, V_DIM)), _const_spec((MLA_HEADS * V_DIM, D_MODEL))],
        out_specs=pl.BlockSpec((tq, D_MODEL), lambda b, i: (b * nq + i, 0)),
        out_shape=jax.ShapeDtypeStruct((bsz * t, D_MODEL), F32),
        scratch_shapes=[pltpu.VMEM((rows, LANES), F32), pltpu.VMEM((rows, LANES), F32),
                        pltpu.VMEM((rows, KV_LORA), F32)],
        compiler_params=_params(("parallel", "arbitrary")),
        name="attn_prompt",
    )(q, kv, wuv, wo)


SAMPLE_SLOTS = 4

def _attn_sample_kernel(pt_ref, q_ref, kvn_ref, cc_hbm, ck_hbm, o_ref,
                        cbuf, kbuf, csem, ksem, m_ref, l_ref, acc_ref, *, pgc, n_pages, tnew):
    b = pl.program_id(0)
    nch = n_pages // pgc
    ahead = SAMPLE_SLOTS - 2
    page_rows = cbuf.shape[2]

    def chunk_copies(seq, ch):
        slot = ch % SAMPLE_SLOTS
        out = []
        for p in range(pgc):
            page = pt_ref[seq * n_pages + ch * pgc + p]
            out.append(pltpu.make_async_copy(cc_hbm.at[page], cbuf.at[slot, p], csem.at[slot]))
            out.append(pltpu.make_async_copy(ck_hbm.at[page], kbuf.at[slot, p], ksem.at[slot]))
        return out

    def start_chunk(ch):
        if ch < nch:
            for cp in chunk_copies(b, ch):
                cp.start()
        else:
            @pl.when(b + 1 < pl.num_programs(0))
            def _():
                for cp in chunk_copies(b + 1, ch - nch):
                    cp.start()

    @pl.when(b == 0)
    def _():
        for ch in range(ahead):
            for cp in chunk_copies(b, ch):
                cp.start()

    _softmax_init(m_ref, l_ref, acc_ref)
    q = q_ref[...]
    rows = q.shape[0]
    all_rows = pl.ds(0, rows)
    q_lat = q[:, :KV_LORA]
    q_pe = q[:, KV_LORA:KV_LORA + QK_ROPE]

    scored = None
    for ch in range(nch):
        start_chunk(ch + ahead)
        for cp in chunk_copies(b, ch):
            cp.wait()
        slot = ch % SAMPLE_SLOTS
        c = cbuf[slot].reshape(pgc * page_rows, KV_LORA).astype(BF16)
        kt = jnp.concatenate([kbuf[slot, p] for p in range(pgc)], axis=1).astype(BF16)
        s = _dot_nt(q_lat, c) + _dot(q_pe, kt)
        if scored is not None:
            _softmax_update(scored[0], scored[1], m_ref, l_ref, acc_ref, all_rows)
        scored = (s, c)
    _softmax_update(scored[0], scored[1], m_ref, l_ref, acc_ref, all_rows)

    kvn = kvn_ref[...].astype(BF16)
    sn = _dot_nt(q, kvn)
    qt = lax.broadcasted_iota(jnp.int32, (rows, tnew), 0) % tnew
    kt_ = lax.broadcasted_iota(jnp.int32, (rows, tnew), 1)
    sn = jnp.where(kt_ <= qt, sn, NEG_INF)
    _softmax_update(sn, kvn[:, :KV_LORA], m_ref, l_ref, acc_ref, all_rows)
    o_ref[...] = _softmax_result(l_ref, acc_ref)


def _attn_sample(page_table, q, kvn, cache_c, cache_kt):
    nb, n_pages = page_table.shape
    tnew = kvn.shape[1]
    rows = q.shape[1]
    page = cache_c.shape[1]
    pgc = _pick_tile(n_pages, 16)
    assert (n_pages // pgc) % SAMPLE_SLOTS == 0, "ring slots are static per chunk, also across sequences"
    kern = functools.partial(_attn_sample_kernel, pgc=pgc, n_pages=n_pages, tnew=tnew)
    grid_spec = pltpu.PrefetchScalarGridSpec(
        num_scalar_prefetch=1,
        grid=(nb,),
        in_specs=[pl.BlockSpec((None, rows, QK_WIDTH), lambda b, pt: (b, 0, 0)),
                  pl.BlockSpec((None, tnew, QK_WIDTH), lambda b, pt: (b, 0, 0)),
                  pl.BlockSpec(memory_space=pl.ANY), pl.BlockSpec(memory_space=pl.ANY)],
        out_specs=pl.BlockSpec((None, rows, KV_LORA), lambda b, pt: (b, 0, 0)),
        scratch_shapes=[pltpu.VMEM((SAMPLE_SLOTS, pgc, page, KV_LORA), F32),
                        pltpu.VMEM((SAMPLE_SLOTS, pgc, QK_ROPE, page), F32),
                        pltpu.SemaphoreType.DMA((SAMPLE_SLOTS,)), pltpu.SemaphoreType.DMA((SAMPLE_SLOTS,)),
                        pltpu.VMEM((rows, LANES), F32), pltpu.VMEM((rows, LANES), F32),
                        pltpu.VMEM((rows, KV_LORA), F32)],
    )
    return pl.pallas_call(
        kern,
        grid_spec=grid_spec,
        out_shape=jax.ShapeDtypeStruct((nb, rows, KV_LORA), F32),
        compiler_params=_params(("arbitrary",)),
        name="attn_sample",
    )(page_table.reshape(-1), q, kvn, cache_c, cache_kt)


def _mla_out_kernel(ol_ref, wuv_ref, wo_ref, o_ref):
    tm = ol_ref.shape[1]
    o_ref[...] = _mla_out(ol_ref[...].reshape(MLA_HEADS * tm, KV_LORA), wuv_ref, wo_ref, tm)


def _mla_out_call(o_lat, wuv, wo):
    n = o_lat.shape[1]
    tm = _pick_tile(n, 256)
    return pl.pallas_call(
        _mla_out_kernel,
        grid=(n // tm,),
        in_specs=[pl.BlockSpec((MLA_HEADS, tm, KV_LORA), lambda i: (0, i, 0)),
                  _const_spec((MLA_HEADS, KV_LORA, V_DIM)), _const_spec((MLA_HEADS * V_DIM, D_MODEL))],
        out_specs=pl.BlockSpec((tm, D_MODEL), lambda i: (i, 0)),
        out_shape=jax.ShapeDtypeStruct((n, D_MODEL), F32),
        compiler_params=_params(("parallel",)),
        name="mla_out",
    )(o_lat, wuv, wo)


def _split_dot(x, w_bf, pieces):
    out = None
    rem = x
    for _ in range(pieces):
        hi = rem.astype(BF16)
        part = _dot(hi, w_bf)
        out = part if out is None else out + part
        rem = rem - hi.astype(F32)
    return out


def _split_dot_lhs(tri_bf, x):
    out = None
    rem = x
    for _ in range(3):
        hi = rem.astype(BF16)
        part = _dot(tri_bf, hi)
        out = part if out is None else out + part
        rem = rem - hi.astype(F32)
    return out


def _expand_heads(x, hg):
    lane_head = lax.broadcasted_iota(jnp.int32, x.shape, 1) // RW_HEAD
    zero = jnp.zeros((), x.dtype)
    return jnp.concatenate([jnp.where(lane_head == hh, x, zero) for hh in range(hg)], axis=0)


def _fold_heads(x, hg):
    c = x.shape[0] // hg
    out = x[0:c]
    for hh in range(1, hg):
        out = out + x[hh * c:(hh + 1) * c]
    return out


def _rwkv_kernel(u_ref, sh_ref, s0_ref, mu_ref, w0_ref, w2a2_ref, a0_ref, g2_ref, kkw_ref, ka_ref, rk_ref,
                 lnw_ref, lnb_ref, ones_ref, wo_ref,
                 ob_ref, sout_ref,
                 state_ref, prev_ref, p_ref, mk_ref, *, c, hg, nseq):
    t = pl.program_id(1)
    tt = u_ref.shape[1]
    nc = tt // c
    ng = RW_HEADS // hg
    w = hg * RW_HEAD
    r_ = hg * c

    blockmask = (lax.broadcasted_iota(jnp.int32, (w, w), 0) // RW_HEAD
                 == lax.broadcasted_iota(jnp.int32, (w, w), 1) // RW_HEAD)

    @pl.when(t == 0)
    def _():
        prev_ref[...] = sh_ref[...]
        for s in range(nseq):
            for g in range(ng):
                s0g = s0_ref[s, :, g * w:(g + 1) * w]
                state_ref[s, g] = jnp.where(blockmask, jnp.concatenate([s0g] * hg, axis=0), 0.0)

    u = u_ref[...].reshape(nseq * tt, RW_COLS)
    row = lax.broadcasted_iota(jnp.int32, u.shape, 0)
    u_prev = pltpu.roll(u, 1, 0)
    for s in range(nseq):
        u_prev = jnp.where(row == s * tt, prev_ref[s], u_prev)
        prev_ref[s] = u[(s + 1) * tt - 1:(s + 1) * tt, :]
    us = u + (u_prev - u) * mu_ref[...]
    r = us[:, 0:RW_DIM]
    k = us[:, RW_DIM:2 * RW_DIM]
    v = us[:, 2 * RW_DIM:3 * RW_DIM]
    o1 = 3 * RW_DIM
    wa = us[:, o1:o1 + DECAY_LORA + AAA_LORA]
    gd = us[:, o1 + DECAY_LORA + AAA_LORA:]
    lane = lax.broadcasted_iota(jnp.int32, wa.shape, 1)
    wa = jnp.where(lane < DECAY_LORA, jnp.tanh(wa), wa)
    lo = _dot(wa.astype(BF16), w2a2_ref[...])
    w_raw = -jax.nn.softplus(-(w0_ref[...] + lo[:, :RW_DIM])) - 0.5
    logw = -jnp.exp(w_raw)
    a = jax.nn.sigmoid(a0_ref[...] + lo[:, RW_DIM:])
    g = _dot(jax.nn.sigmoid(gd).astype(BF16), g2_ref[...])
    ones_bd = ones_ref[...]
    kk = k * kkw_ref[...]
    kk = kk / jnp.maximum(jnp.sqrt(_split_dot(kk * kk, ones_bd, 1)), 1e-12)
    k2 = k * (1.0 + (a - 1.0) * ka_ref[...])

    tri = (lax.broadcasted_iota(jnp.int32, (c, c), 1) <= lax.broadcasted_iota(jnp.int32, (c, c), 0)).astype(BF16)
    rr = lax.broadcasted_iota(jnp.int32, (r_, r_), 0)
    cc = lax.broadcasted_iota(jnp.int32, (r_, r_), 1)
    strict_bd = (rr // c == cc // c) & (cc % c < rr % c)
    eye = (rr == cc).astype(F32)
    incl_wide = (lax.broadcasted_iota(jnp.int32, (c, r_), 1) % c
                 <= lax.broadcasted_iota(jnp.int32, (c, r_), 0))

    pre = {}
    for s in range(nseq):
        for ci in range(nc):
            rows = slice(s * tt + ci * c, s * tt + (ci + 1) * c)
            lw = logw[rows]
            cs = _split_dot_lhs(tri, lw)
            e_cur = jnp.exp(cs)
            e_inv = jnp.exp(-cs)
            kkc = kk[rows]
            at = (-kkc * jnp.exp(cs - lw)).astype(BF16)
            bt = (kkc * a[rows] * e_inv).astype(BF16)
            kt = (k2[rows] * e_inv).astype(BF16)
            rt = (r[rows] * e_cur).astype(BF16)
            vb = v[rows].astype(BF16)
            for g_ in range(ng):
                sl = slice(g_ * w, (g_ + 1) * w)
                a_, b_, k_, r__, v_ = at[:, sl], bt[:, sl], kt[:, sl], rt[:, sl], vb[:, sl]
                ax, bx, kx, vx = (_expand_heads(z, hg) for z in (a_, b_, k_, v_))
                n1 = _dot_nt(jnp.concatenate([ax, r__], axis=0), jnp.concatenate([bx, kx], axis=0))
                m_ab = jnp.where(strict_bd, n1[:r_, :r_], 0.0)
                m_ak = jnp.where(strict_bd, n1[:r_, r_:], 0.0).astype(BF16)
                a_rb = jnp.where(incl_wide, n1[r_:, :r_], 0.0).astype(BF16)
                a_rk = jnp.where(incl_wide, n1[r_:, r_:], 0.0).astype(BF16)
                idx = (s * nc + ci) * ng + g_
                p_ref[idx] = eye + m_ab
                mk_ref[idx] = m_ab.astype(BF16)
                pre[s, ci, g_] = dict(
                    idx=idx,
                    wv=_dot(m_ak, vx),
                    yv=_dot(a_rk, vx),
                    ar=jnp.concatenate([a_, r__], axis=0), a_rb=a_rb, v=v_,
                    bk=jnp.concatenate([b_, k_], axis=0), p_end=e_cur[c - 1:c, sl])

    def inverse_level(_, carry):
        for idx in range(nseq * nc * ng):
            mk = mk_ref[idx]
            mk2 = _dot(mk, mk).astype(BF16)
            p = p_ref[idx]
            p_ref[idx] = p + _dot(p.astype(BF16), mk2)
            mk_ref[idx] = mk2
        return carry

    levels = max(c.bit_length() - 2, 0)
    lax.fori_loop(0, levels, inverse_level, 0)

    y_rows = []
    for s in range(nseq):
        for ci in range(nc):
            ys = []
            for g_ in range(ng):
                z = pre[s, ci, g_]
                s_f32 = state_ref[s, g_]
                sa = _dot_nt(z["ar"], s_f32.astype(BF16))
                wx = _expand_heads(sa[:c], hg) + z["wv"]
                ux = _dot(p_ref[z["idx"]].astype(BF16), wx.astype(BF16))
                u_ = _fold_heads(ux, hg)
                ys.append(sa[c:] + _dot(z["a_rb"], ux.astype(BF16)) + z["yv"])
                ds = _dot_tn(jnp.concatenate([u_.astype(BF16), z["v"]], axis=0), z["bk"])
                state_ref[s, g_] = (s_f32 + jnp.where(blockmask, ds, 0.0)) * z["p_end"]
            y_rows.append(ys[0] if ng == 1 else jnp.concatenate(ys, axis=1))
    y = y_rows[0] if len(y_rows) == 1 else jnp.concatenate(y_rows, axis=0)

    inv_n = 1.0 / RW_HEAD
    mean = _split_dot(y, ones_bd, 2) * inv_n
    d = y - mean
    var = _split_dot(d * d, ones_bd, 1) * inv_n
    yn = d * lax.rsqrt(var + GN_EPS) * lnw_ref[...] + lnb_ref[...]
    bonus = _split_dot(r * k2 * rk_ref[...], ones_bd, 1) * v
    ob_ref[...] = _dot(((yn + bonus) * g).astype(BF16), wo_ref[...]).reshape(nseq, tt, D_MODEL)

    @pl.when(t == pl.num_programs(1) - 1)
    def _():
        for s in range(nseq):
            for g_ in range(ng):
                sout_ref[s, :, g_ * w:(g_ + 1) * w] = _fold_heads(state_ref[s, g_], hg)


def _rwkv(u_rw, shift_in, s0, w, bsz, t):
    if t >= 64:
        c, hg, nseq = 64, 4, _pick_tile(bsz, 2)
        tt = _pick_tile(t, 256)
    else:
        c, hg, nseq = t, RW_HEADS, _pick_tile(bsz, 8)
        tt = t
    nt = t // tt
    ng = RW_HEADS // hg
    wd_ = hg * RW_HEAD
    n_inst = nseq * (tt // c) * ng
    kern = functools.partial(_rwkv_kernel, c=c, hg=hg, nseq=nseq)
    vec = lambda n: _const_spec((1, n))
    o_b, s_new = pl.pallas_call(
        kern,
        grid=(bsz // nseq, nt),
        in_specs=[pl.BlockSpec((nseq, tt, RW_COLS), lambda b, i: (b, i, 0)),
                  pl.BlockSpec((nseq, 1, RW_COLS), lambda b, i: (b, 0, 0)),
                  pl.BlockSpec((nseq, RW_HEAD, RW_DIM), lambda b, i: (b, 0, 0)),
                  vec(RW_COLS), vec(RW_DIM), _const_spec((DECAY_LORA + AAA_LORA, 2 * RW_DIM)), vec(RW_DIM),
                  _const_spec((GATE_LORA, RW_DIM)), vec(RW_DIM), vec(RW_DIM), vec(RW_DIM), vec(RW_DIM),
                  vec(RW_DIM), _const_spec((RW_DIM, RW_DIM)), _const_spec((RW_DIM, D_MODEL))],
        out_specs=[pl.BlockSpec((nseq, tt, D_MODEL), lambda b, i: (b, i, 0)),
                   pl.BlockSpec((nseq, RW_HEAD, RW_DIM), lambda b, i: (b, 0, 0))],
        out_shape=[jax.ShapeDtypeStruct((bsz, t, D_MODEL), F32),
                   jax.ShapeDtypeStruct((bsz, RW_HEAD, RW_DIM), F32)],
        scratch_shapes=[pltpu.VMEM((nseq, ng, wd_, wd_), F32), pltpu.VMEM((nseq, 1, RW_COLS), F32),
                        pltpu.VMEM((n_inst, hg * c, hg * c), F32), pltpu.VMEM((n_inst, hg * c, hg * c), BF16)],
        compiler_params=_params(("parallel", "arbitrary")),
        name="rwkv",
    )(u_rw.reshape(bsz, t, RW_COLS), shift_in, s0, w["rw_mu"], w["rw_w0"], w["rw_w2a2"], w["rw_a0"], w["rw_g2"],
      w["rw_kk"], w["rw_ka"], w["rw_rk"], w["rw_ln_w"], w["rw_ln_b"], w["ones_bd"], w["w_o_rw"])
    return o_b.reshape(bsz * t, D_MODEL), s_new


def _merge_kernel(x1_ref, oa_ref, ob_ref, gate_ref, wout_ref, n2_ref, wg_ref, wu_ref, wd_ref, nf_ref, y_ref):
    mixed = (gate_ref[:, :D_MODEL] * oa_ref[...] + gate_ref[:, D_MODEL:] * ob_ref[...]).astype(BF16)
    x2 = x1_ref[...] + _dot(mixed, wout_ref[...])
    hn = _rmsnorm(x2, n2_ref[...]).astype(BF16)
    x3 = x2 + 0.5 * _swiglu(hn, wg_ref, wu_ref, wd_ref)
    y_ref[...] = _rmsnorm(x3, nf_ref[...])


def _merge(x1, oa, ob, gate, wout, n2, wg, wu, wd, nf):
    n = x1.shape[0]
    tm = _pick_tile(n, 256)
    row = lambda w_: pl.BlockSpec((tm, w_), lambda i: (i, 0))
    return pl.pallas_call(
        _merge_kernel,
        grid=(n // tm,),
        in_specs=[row(D_MODEL), row(D_MODEL), row(D_MODEL), row(2 * D_MODEL),
                  _const_spec((D_MODEL, D_MODEL)), _const_spec((1, D_MODEL)), _const_spec((D_MODEL, D_FF)),
                  _const_spec((D_MODEL, D_FF)), _const_spec((D_FF, D_MODEL)), _const_spec((1, D_MODEL))],
        out_specs=row(D_MODEL),
        out_shape=jax.ShapeDtypeStruct((n, D_MODEL), F32),
        compiler_params=_params(("parallel",)),
        name="merge_ffn2",
    )(x1, oa, ob, gate, wout, n2, wg, wu, wd, nf)


def _rope_tables(pos):
    inv = 1.0 / (ROPE_THETA ** (jnp.arange(0, QK_ROPE, 2, dtype=F32) / QK_ROPE))
    ang = pos.astype(F32)[:, None] * inv[None, :]
    pad = jnp.zeros((pos.shape[0], ROPE_PAD - QK_ROPE), F32)
    cos, sin = jnp.cos(ang), jnp.sin(ang)
    return jnp.concatenate([cos, cos, pad], axis=1), jnp.concatenate([sin, sin, pad], axis=1)


def _rot_cols(wm):
    half = QK_ROPE // 2
    return jnp.concatenate([-wm[..., half:], wm[..., :half]], axis=-1)


def _pad_cols(wm, width):
    return jnp.pad(wm, [(0, 0)] * (wm.ndim - 1) + [(0, width - wm.shape[-1])])


def _prep_weights(p):
    w_in = p["w_in"]
    w_kpe = w_in[:, Q_LORA + KV_LORA:MLA_COLS]
    w_mla = jnp.concatenate([w_in[:, :Q_LORA + KV_LORA], _pad_cols(w_kpe, ROPE_PAD),
                             _pad_cols(_rot_cols(w_kpe), ROPE_PAD)], axis=1)
    w_qb = p["w_qb"].reshape(Q_LORA, MLA_HEADS, QK_NOPE + QK_ROPE)
    w_nope = w_qb[:, :, :QK_NOPE].reshape(Q_LORA, MLA_HEADS * QK_NOPE)
    w_pe = w_qb[:, :, QK_NOPE:]
    w_uk = jnp.transpose(p["w_uk"], (1, 2, 0))
    eye = jnp.eye(MLA_HEADS, dtype=F32)
    w_uk_bd = (eye[:, None, :, None] * w_uk[:, :, None, :]).reshape(MLA_HEADS * QK_NOPE, MLA_HEADS * KV_LORA)
    zeros = jnp.zeros((DECAY_LORA, RW_DIM), F32)
    w2a2 = jnp.concatenate([jnp.concatenate([p["rw_w2"], zeros], axis=1),
                            jnp.concatenate([zeros, p["rw_a2"]], axis=1)], axis=0)
    head = jnp.arange(RW_DIM) // RW_HEAD
    row = lambda v: v.reshape(1, -1)
    return {
        "w_mla": w_mla.astype(BF16),
        "w_rw": w_in[:, MLA_COLS:MLA_COLS + RW_COLS].astype(BF16),
        "w_gate": w_in[:, MLA_COLS + RW_COLS:].astype(BF16),
        "q_norm": row(p["q_norm"]), "kv_norm": row(p["kv_norm"]),
        "w_nope": w_nope.astype(BF16),
        "w_pe": _pad_cols(w_pe, ROPE_PAD).reshape(Q_LORA, MLA_HEADS * ROPE_PAD).astype(BF16),
        "w_pe_rot": _pad_cols(_rot_cols(w_pe), ROPE_PAD).reshape(Q_LORA, MLA_HEADS * ROPE_PAD).astype(BF16),
        "w_uk_bd": w_uk_bd.astype(BF16),
        "w_uv": jnp.transpose(p["w_uv"], (1, 0, 2)).astype(BF16),
        "w_o_mla": p["w_o_mla"].astype(BF16),
        "rw_mu": row(p["rw_mu"]), "rw_w0": row(p["rw_w0"]), "rw_w2a2": w2a2.astype(BF16),
        "rw_a0": row(p["rw_a0"]), "rw_g2": p["rw_g2"].astype(BF16), "rw_kk": row(p["rw_kk"]),
        "rw_ka": row(p["rw_ka"]), "rw_rk": row(p["rw_rk"]), "rw_ln_w": row(p["rw_ln_w"]),
        "rw_ln_b": row(p["rw_ln_b"]),
        "ones_bd": (head[:, None] == head[None, :]).astype(BF16),
        "w_o_rw": p["w_o_rw"].astype(BF16),
        "w_out": p["w_out"].astype(BF16),
        "ffn1_norm": row(p["ffn1_norm"]), "mix_norm": row(p["mix_norm"]), "ffn2_norm": row(p["ffn2_norm"]),
        "ffn1_wg": p["ffn1_wg"].astype(BF16), "ffn1_wu": p["ffn1_wu"].astype(BF16),
        "ffn1_wd": p["ffn1_wd"].astype(BF16),
        "ffn2_wg": p["ffn2_wg"].astype(BF16), "ffn2_wu": p["ffn2_wu"].astype(BF16),
        "ffn2_wd": p["ffn2_wd"].astype(BF16),
    }


def _state_to_lanes(s):
    b = s.shape[0]
    return jnp.transpose(s, (0, 2, 1, 3)).reshape(b, RW_HEAD, RW_DIM)


def _state_from_lanes(s):
    b = s.shape[0]
    return jnp.transpose(s.reshape(b, RW_HEAD, RW_HEADS, RW_HEAD), (0, 2, 1, 3))


def _layer(x, cos, sin, shift_in, s0, w, final_norm, attend):
    bsz, t, _ = x.shape
    n = bsz * t
    x1, h = _ffn1(x.reshape(n, D_MODEL), w["ffn1_norm"], w["ffn1_wg"], w["ffn1_wu"], w["ffn1_wd"], w["mix_norm"])
    q, kv, ckv, kpe, u_rw, gate = _proj(h, cos, sin, w)
    o_a = attend(q, kv)
    o_b, s_new = _rwkv(u_rw, shift_in.reshape(bsz, 1, RW_COLS), _state_to_lanes(s0), w, bsz, t)
    y = _merge(x1, o_a, o_b, gate, w["w_out"], w["ffn2_norm"], w["ffn2_wg"], w["ffn2_wu"], w["ffn2_wd"],
               final_norm.reshape(1, D_MODEL))
    return (y.reshape(bsz, t, D_MODEL), ckv.reshape(bsz, t, KV_LORA), kpe.reshape(bsz, t, QK_ROPE),
            _state_from_lanes(s_new), u_rw.reshape(bsz, t, RW_COLS)[:, -1])


def kernel(x_prompt, x_sample, cache_ckv, cache_kpe, state_wkv, state_shift, page_table, ffn1_norm, ffn1_wg,
           ffn1_wu, ffn1_wd, mix_norm, w_in, q_norm, kv_norm, w_qb, w_uk, w_uv, w_o_mla, rw_mu, rw_w0, rw_w2,
           rw_a0, rw_a2, rw_g2, rw_kk, rw_ka, rw_rk, rw_ln_w, rw_ln_b, w_o_rw, w_out, ffn2_norm, ffn2_wg,
           ffn2_wu, ffn2_wd, final_norm):
    depth = w_in.shape[0]
    assert depth == 1, "the final norm is fused into the layer's last stage"
    layer_params = dict(ffn1_norm=ffn1_norm, ffn1_wg=ffn1_wg, ffn1_wu=ffn1_wu, ffn1_wd=ffn1_wd, mix_norm=mix_norm,
                        w_in=w_in, q_norm=q_norm, kv_norm=kv_norm, w_qb=w_qb, w_uk=w_uk, w_uv=w_uv,
                        w_o_mla=w_o_mla, rw_mu=rw_mu, rw_w0=rw_w0, rw_w2=rw_w2, rw_a0=rw_a0, rw_a2=rw_a2,
                        rw_g2=rw_g2, rw_kk=rw_kk, rw_ka=rw_ka, rw_rk=rw_rk, rw_ln_w=rw_ln_w, rw_ln_b=rw_ln_b,
                        w_o_rw=w_o_rw, w_out=w_out, ffn2_norm=ffn2_norm, ffn2_wg=ffn2_wg, ffn2_wu=ffn2_wu,
                        ffn2_wd=ffn2_wd)
    w = _prep_weights({k: v[0] for k, v in layer_params.items()})

    bp, tp, _ = x_prompt.shape
    bs, ts, _ = x_sample.shape
    n_pages = page_table.shape[1]
    past_len = n_pages * cache_ckv.shape[2]

    cos_p, sin_p = _rope_tables(jnp.arange(tp, dtype=jnp.int32))
    attend_p = lambda q, kv: _attn_prompt(q, kv, w["w_uv"], w["w_o_mla"], bp, tp)
    out_p = _layer(x_prompt, cos_p, sin_p, jnp.zeros((bp, RW_COLS), F32),
                   jnp.zeros((bp, RW_HEADS, RW_HEAD, RW_HEAD), F32), w, final_norm, attend_p)

    n_s = bs * ts
    tile_s = _pick_tile(n_s, 512)
    pos_s = past_len + jnp.arange(tile_s, dtype=jnp.int32) % ts
    cos_s, sin_s = _rope_tables(pos_s)

    def attend_s(q, kv):
        q_b = jnp.transpose(q.reshape(MLA_HEADS, bs, ts, QK_WIDTH), (1, 0, 2, 3)).reshape(bs, MLA_HEADS * ts, QK_WIDTH)
        kv_new = kv.astype(F32).reshape(bs, ts, QK_WIDTH)
        o_lat = _attn_sample(page_table, q_b, kv_new, cache_ckv[0], jnp.swapaxes(cache_kpe[0], 1, 2))
        o_lat = jnp.transpose(o_lat.reshape(bs, MLA_HEADS, ts, KV_LORA), (1, 0, 2, 3)).reshape(MLA_HEADS, n_s, KV_LORA)
        return _mla_out_call(o_lat, w["w_uv"], w["w_o_mla"])

    out_s = _layer(x_sample, cos_s, sin_s, state_shift[0], state_wkv[0], w, final_norm, attend_s)

    y_p, ckv_p, kpe_p, wkv_p, sh_p = out_p
    y_s, ckv_s, kpe_s, wkv_s, sh_s = out_s
    stack = lambda z: z[None]
    return (y_p, y_s, stack(ckv_p), stack(kpe_p), stack(wkv_p), stack(sh_p),
            stack(ckv_s), stack(kpe_s), stack(wkv_s), stack(sh_s))
```

```python
import functools

import jax
import jax.numpy as jnp
from jax import lax
from jax.experimental import pallas as pl
from jax.experimental.pallas import tpu as pltpu

F32 = jnp.float32
BF16 = jnp.bfloat16

D_MODEL = 1024
MLA_HEADS = 8
QK_NOPE = 64
QK_ROPE = 32
V_DIM = 64
Q_LORA = 384
KV_LORA = 256
ROPE_THETA = 10000.0
SM_SCALE = (QK_NOPE + QK_ROPE) ** -0.5
RW_HEADS = 8
RW_HEAD = 64
RW_DIM = RW_HEADS * RW_HEAD
DECAY_LORA = 64
AAA_LORA = 64
GATE_LORA = 128
GN_EPS = RW_HEAD * 1e-5
D_FF = 2816
RMS_EPS = 1e-6
NEG_INF = -1e30
MLA_COLS = Q_LORA + KV_LORA + QK_ROPE
RW_COLS = 3 * RW_DIM + DECAY_LORA + AAA_LORA + GATE_LORA

LANES = 128
ROPE_PAD = LANES
QK_WIDTH = KV_LORA + ROPE_PAD
FF_CHUNKS = 2
VMEM_LIMIT = 56 * 1024 * 1024

_NT = (((1,), (1,)), ((), ()))
_TN = (((0,), (0,)), ((), ()))


def _pick_tile(n, pref):
    t = pref
    while n % t:
        t //= 2
    return t


def _const_spec(shape):
    nd = len(shape)
    return pl.BlockSpec(shape, lambda *_: (0,) * nd, pipeline_mode=pl.Buffered(1))


def _params(sem):
    return pltpu.CompilerParams(dimension_semantics=sem, vmem_limit_bytes=VMEM_LIMIT)


def _dot(a, b):
    return jnp.dot(a, b, preferred_element_type=F32)


def _dot_nt(a, b):
    return lax.dot_general(a, b, _NT, preferred_element_type=F32)


def _dot_tn(a, b):
    return lax.dot_general(a, b, _TN, preferred_element_type=F32)


def _rmsnorm(x, g):
    return x * lax.rsqrt(jnp.mean(x * x, axis=-1, keepdims=True) + RMS_EPS) * g


def _swiglu(hn, wg_ref, wu_ref, wd_ref):
    fc = D_FF // FF_CHUNKS
    acc = None
    for c in range(FF_CHUNKS):
        g = _dot(hn, wg_ref[:, c * fc:(c + 1) * fc])
        u = _dot(hn, wu_ref[:, c * fc:(c + 1) * fc])
        act = (g * jax.nn.sigmoid(g) * u).astype(BF16)
        part = _dot(act, wd_ref[c * fc:(c + 1) * fc, :])
        acc = part if acc is None else acc + part
    return acc


def _ffn1_kernel(x_ref, n1_ref, wg_ref, wu_ref, wd_ref, nm_ref, x1_ref, h_ref):
    x = x_ref[...]
    hn = _rmsnorm(x, n1_ref[...]).astype(BF16)
    x1 = x + 0.5 * _swiglu(hn, wg_ref, wu_ref, wd_ref)
    x1_ref[...] = x1
    h_ref[...] = _rmsnorm(x1, nm_ref[...]).astype(BF16)


def _ffn1(x, n1, wg, wu, wd, nm):
    n = x.shape[0]
    tm = _pick_tile(n, 512)
    row = lambda w: pl.BlockSpec((tm, w), lambda i: (i, 0))
    return pl.pallas_call(
        _ffn1_kernel,
        grid=(n // tm,),
        in_specs=[row(D_MODEL), _const_spec((1, D_MODEL)), _const_spec((D_MODEL, D_FF)),
                  _const_spec((D_MODEL, D_FF)), _const_spec((D_FF, D_MODEL)), _const_spec((1, D_MODEL))],
        out_specs=[row(D_MODEL), row(D_MODEL)],
        out_shape=[jax.ShapeDtypeStruct((n, D_MODEL), F32), jax.ShapeDtypeStruct((n, D_MODEL), BF16)],
        compiler_params=_params(("parallel",)),
        name="ffn1",
    )(x, n1, wg, wu, wd, nm)


def _proj_kernel(h_ref, cos_ref, sin_ref, wmla_ref, wrw_ref, wgate_ref, qn_ref, kvn_ref,
                 wnope_ref, wpe_ref, wper_ref, wuk_ref,
                 q_ref, kv_ref, ckv_ref, kpe_ref, urw_ref, gate_ref):
    h = h_ref[...]
    cos = cos_ref[...]
    sin = sin_ref[...]
    u = _dot(h, wmla_ref[...])
    qa = u[:, :Q_LORA]
    c0 = Q_LORA + KV_LORA
    ckv = _rmsnorm(u[:, Q_LORA:c0], kvn_ref[...])
    kpe = u[:, c0:c0 + ROPE_PAD] * cos + u[:, c0 + ROPE_PAD:] * sin
    ckv_ref[...] = ckv
    kpe_ref[...] = kpe[:, :QK_ROPE]
    kv_ref[...] = jnp.concatenate([ckv, kpe], axis=1).astype(BF16)

    qn = _rmsnorm(qa, qn_ref[...]).astype(BF16)
    q_nope = _dot(qn, wnope_ref[...]).astype(BF16)
    q_lat = _dot(q_nope, wuk_ref[...])
    qp = _dot(qn, wpe_ref[...])
    qr = _dot(qn, wper_ref[...])
    for hh in range(MLA_HEADS):
        ql = q_lat[:, hh * KV_LORA:(hh + 1) * KV_LORA] * SM_SCALE
        sl = slice(hh * ROPE_PAD, (hh + 1) * ROPE_PAD)
        pe = (qp[:, sl] * cos + qr[:, sl] * sin) * SM_SCALE
        q_ref[hh] = jnp.concatenate([ql, pe], axis=1).astype(BF16)

    urw_ref[...] = _dot(h, wrw_ref[...])
    gate_ref[...] = jax.nn.sigmoid(_dot(h, wgate_ref[...]))


def _proj(h, cos, sin, w):
    n = h.shape[0]
    tm = _pick_tile(cos.shape[0], _pick_tile(n, 512))
    nrb = cos.shape[0] // tm
    row = lambda wd_: pl.BlockSpec((tm, wd_), lambda i: (i, 0))
    rope = pl.BlockSpec((tm, ROPE_PAD), lambda i: (i % nrb, 0))
    wmla_w = Q_LORA + KV_LORA + 2 * ROPE_PAD
    return pl.pallas_call(
        _proj_kernel,
        grid=(n // tm,),
        in_specs=[row(D_MODEL), rope, rope,
                  _const_spec((D_MODEL, wmla_w)), _const_spec((D_MODEL, RW_COLS)),
                  _const_spec((D_MODEL, 2 * D_MODEL)), _const_spec((1, Q_LORA)), _const_spec((1, KV_LORA)),
                  _const_spec((Q_LORA, MLA_HEADS * QK_NOPE)), _const_spec((Q_LORA, MLA_HEADS * ROPE_PAD)),
                  _const_spec((Q_LORA, MLA_HEADS * ROPE_PAD)),
                  _const_spec((MLA_HEADS * QK_NOPE, MLA_HEADS * KV_LORA))],
        out_specs=[pl.BlockSpec((MLA_HEADS, tm, QK_WIDTH), lambda i: (0, i, 0)),
                   row(QK_WIDTH), row(KV_LORA), row(QK_ROPE), row(RW_COLS), row(2 * D_MODEL)],
        out_shape=[jax.ShapeDtypeStruct((MLA_HEADS, n, QK_WIDTH), BF16),
                   jax.ShapeDtypeStruct((n, QK_WIDTH), BF16),
                   jax.ShapeDtypeStruct((n, KV_LORA), F32),
                   jax.ShapeDtypeStruct((n, QK_ROPE), F32),
                   jax.ShapeDtypeStruct((n, RW_COLS), F32),
                   jax.ShapeDtypeStruct((n, 2 * D_MODEL), F32)],
        compiler_params=_params(("parallel",)),
        name="proj",
    )(h, cos, sin, w["w_mla"], w["w_rw"], w["w_gate"], w["q_norm"], w["kv_norm"],
      w["w_nope"], w["w_pe"], w["w_pe_rot"], w["w_uk_bd"])


def _lane_tile(x, n):
    if n <= LANES:
        return x[:, :n]
    return jnp.concatenate([x] * (n // LANES), axis=1)


def _softmax_update(s, v, m_ref, l_ref, acc_ref, rows):
    m_prev = m_ref[rows, :]
    m_new = jnp.maximum(m_prev, jnp.max(s, axis=-1, keepdims=True))
    alpha = jnp.exp(m_prev - m_new)
    p = jnp.exp(s - _lane_tile(m_new, s.shape[1]))
    l_ref[rows, :] = alpha * l_ref[rows, :] + jnp.sum(p, axis=-1, keepdims=True)
    acc_ref[rows, :] = _lane_tile(alpha, KV_LORA) * acc_ref[rows, :] + _dot(p.astype(BF16), v)
    m_ref[rows, :] = m_new


def _softmax_init(m_ref, l_ref, acc_ref):
    m_ref[...] = jnp.full(m_ref.shape, NEG_INF, F32)
    l_ref[...] = jnp.zeros(l_ref.shape, F32)
    acc_ref[...] = jnp.zeros(acc_ref.shape, F32)


def _softmax_result(l_ref, acc_ref):
    return acc_ref[...] / _lane_tile(l_ref[...], KV_LORA)


def _mla_out(o_lat, wuv_ref, wo_ref, rows):
    heads = [_dot(o_lat[hh * rows:(hh + 1) * rows].astype(BF16), wuv_ref[hh]).astype(BF16)
             for hh in range(MLA_HEADS)]
    return _dot(jnp.concatenate(heads, axis=1), wo_ref[...])


ATTN_SPLIT = 2


def _attn_prompt_kernel(q_ref, kv_ref, wuv_ref, wo_ref, o_ref, m_ref, l_ref, acc_ref, *, tq, tk):
    i = pl.program_id(1)
    hs = MLA_HEADS // ATTN_SPLIT
    sub = hs * tq
    _softmax_init(m_ref, l_ref, acc_ref)
    n_full = (i * tq) // tk

    def chunks(starts, diagonal_last):
        kvs = [kv_ref[pl.ds(k0, tk), :] for k0 in starts]
        if diagonal_last:
            delta = (lax.broadcasted_iota(jnp.int32, (sub, tk), 1)
                     - lax.broadcasted_iota(jnp.int32, (sub, tk), 0) % tq)
            visible = delta <= i * tq - starts[-1]
        for g in range(ATTN_SPLIT):
            q = q_ref[g * hs:(g + 1) * hs].reshape(sub, QK_WIDTH)
            scores = [_dot_nt(q, kv) for kv in kvs]
            if diagonal_last:
                scores[-1] = jnp.where(visible, scores[-1], NEG_INF)
            for kv, s in zip(kvs, scores):
                _softmax_update(s, kv[:, :KV_LORA], m_ref, l_ref, acc_ref, pl.ds(g * sub, sub))

    def pair_step(j, carry):
        k0 = pl.multiple_of(2 * j * tk, 2 * tk)
        chunks([k0, pl.multiple_of(k0 + tk, tk)], False)
        return carry

    lax.fori_loop(0, n_full // 2, pair_step, 0)
    k_diag = pl.multiple_of(n_full * tk, tk)

    @pl.when(n_full % 2 == 1)
    def _():
        chunks([pl.multiple_of(k_diag - tk, tk), k_diag], True)

    @pl.when(n_full % 2 == 0)
    def _():
        chunks([k_diag], True)

    o_ref[...] = _mla_out(_softmax_result(l_ref, acc_ref), wuv_ref, wo_ref, tq)


def _attn_prompt(q, kv, wuv, wo, bsz, t):
    tq = _pick_tile(t, 256)
    tk = _pick_tile(t, 512)
    nq = t // tq
    rows = MLA_HEADS * tq
    kern = functools.partial(_attn_prompt_kernel, tq=tq, tk=tk)
    return pl.pallas_call(
        kern,
        grid=(bsz, nq),
        in_specs=[pl.BlockSpec((MLA_HEADS, tq, QK_WIDTH), lambda b, i: (0, b * nq + i, 0)),
                  pl.BlockSpec((t, QK_WIDTH), lambda b, i: (b, 0)),
                  _const_spec((MLA_HEADS, KV_LORA, V_DIM)), _const_spec((MLA_HEADS * V_DIM, D_MODEL))],
        out_specs=pl.BlockSpec((tq, D_MODEL), lambda b, i: (b * nq + i, 0)),
        out_shape=jax.ShapeDtypeStruct((bsz * t, D_MODEL), F32),
        scratch_shapes=[pltpu.VMEM((rows, LANES), F32), pltpu.VMEM((rows, LANES), F32),
                        pltpu.VMEM((rows, KV_LORA), F32)],
        compiler_params=_params(("parallel", "arbitrary")),
        name="attn_prompt",
    )(q, kv, wuv, wo)


SAMPLE_SLOTS = 4

def _attn_sample_kernel(pt_ref, q_ref, kvn_ref, cc_hbm, ck_hbm, o_ref,
                        cbuf, kbuf, csem, ksem, m_ref, l_ref, acc_ref, *, pgc, n_pages, tnew):
    b = pl.program_id(0)
    nch = n_pages // pgc
    ahead = SAMPLE_SLOTS - 2
    page_rows = cbuf.shape[2]

    def chunk_copies(seq, ch):
        slot = ch % SAMPLE_SLOTS
        out = []
        for p in range(pgc):
            page = pt_ref[seq * n_pages + ch * pgc + p]
            out.append(pltpu.make_async_copy(cc_hbm.at[page], cbuf.at[slot, p], csem.at[slot]))
            out.append(pltpu.make_async_copy(ck_hbm.at[page], kbuf.at[slot, p], ksem.at[slot]))
        return out

    def start_chunk(ch):
        if ch < nch:
            for cp in chunk_copies(b, ch):
                cp.start()
        else:
            @pl.when(b + 1 < pl.num_programs(0))
            def _():
                for cp in chunk_copies(b + 1, ch - nch):
                    cp.start()

    @pl.when(b == 0)
    def _():
        for ch in range(ahead):
            for cp in chunk_copies(b, ch):
                cp.start()

    _softmax_init(m_ref, l_ref, acc_ref)
    q = q_ref[...]
    rows = q.shape[0]
    all_rows = pl.ds(0, rows)
    q_lat = q[:, :KV_LORA]
    q_pe = q[:, KV_LORA:KV_LORA + QK_ROPE]

    scored = None
    for ch in range(nch):
        start_chunk(ch + ahead)
        for cp in chunk_copies(b, ch):
            cp.wait()
        slot = ch % SAMPLE_SLOTS
        c = cbuf[slot].reshape(pgc * page_rows, KV_LORA).astype(BF16)
        kt = jnp.concatenate([kbuf[slot, p] for p in range(pgc)], axis=1).astype(BF16)
        s = _dot_nt(q_lat, c) + _dot(q_pe, kt)
        if scored is not None:
            _softmax_update(scored[0], scored[1], m_ref, l_ref, acc_ref, all_rows)
        scored = (s, c)
    _softmax_update(scored[0], scored[1], m_ref, l_ref, acc_ref, all_rows)

    kvn = kvn_ref[...].astype(BF16)
    sn = _dot_nt(q, kvn)
    qt = lax.broadcasted_iota(jnp.int32, (rows, tnew), 0) % tnew
    kt_ = lax.broadcasted_iota(jnp.int32, (rows, tnew), 1)
    sn = jnp.where(kt_ <= qt, sn, NEG_INF)
    _softmax_update(sn, kvn[:, :KV_LORA], m_ref, l_ref, acc_ref, all_rows)
    o_ref[...] = _softmax_result(l_ref, acc_ref)


def _attn_sample(page_table, q, kvn, cache_c, cache_kt):
    nb, n_pages = page_table.shape
    tnew = kvn.shape[1]
    rows = q.shape[1]
    page = cache_c.shape[1]
    pgc = _pick_tile(n_pages, 32)
    assert (n_pages // pgc) % SAMPLE_SLOTS == 0, "ring slots are static per chunk, also across sequences"
    kern = functools.partial(_attn_sample_kernel, pgc=pgc, n_pages=n_pages, tnew=tnew)
    grid_spec = pltpu.PrefetchScalarGridSpec(
        num_scalar_prefetch=1,
        grid=(nb,),
        in_specs=[pl.BlockSpec((None, rows, QK_WIDTH), lambda b, pt: (b, 0, 0)),
                  pl.BlockSpec((None, tnew, QK_WIDTH), lambda b, pt: (b, 0, 0)),
                  pl.BlockSpec(memory_space=pl.ANY), pl.BlockSpec(memory_space=pl.ANY)],
        out_specs=pl.BlockSpec((None, rows, KV_LORA), lambda b, pt: (b, 0, 0)),
        scratch_shapes=[pltpu.VMEM((SAMPLE_SLOTS, pgc, page, KV_LORA), F32),
                        pltpu.VMEM((SAMPLE_SLOTS, pgc, QK_ROPE, page), F32),
                        pltpu.SemaphoreType.DMA((SAMPLE_SLOTS,)), pltpu.SemaphoreType.DMA((SAMPLE_SLOTS,)),
                        pltpu.VMEM((rows, LANES), F32), pltpu.VMEM((rows, LANES), F32),
                        pltpu.VMEM((rows, KV_LORA), F32)],
    )
    return pl.pallas_call(
        kern,
        grid_spec=grid_spec,
        out_shape=jax.ShapeDtypeStruct((nb, rows, KV_LORA), F32),
        compiler_params=_params(("arbitrary",)),
        name="attn_sample",
    )(page_table.reshape(-1), q, kvn, cache_c, cache_kt)


def _mla_out_kernel(ol_ref, wuv_ref, wo_ref, o_ref):
    tm = ol_ref.shape[1]
    o_ref[...] = _mla_out(ol_ref[...].reshape(MLA_HEADS * tm, KV_LORA), wuv_ref, wo_ref, tm)


def _mla_out_call(o_lat, wuv, wo):
    n = o_lat.shape[1]
    tm = _pick_tile(n, 256)
    return pl.pallas_call(
        _mla_out_kernel,
        grid=(n // tm,),
        in_specs=[pl.BlockSpec((MLA_HEADS, tm, KV_LORA), lambda i: (0, i, 0)),
                  _const_spec((MLA_HEADS, KV_LORA, V_DIM)), _const_spec((MLA_HEADS * V_DIM, D_MODEL))],
        out_specs=pl.BlockSpec((tm, D_MODEL), lambda i: (i, 0)),
        out_shape=jax.ShapeDtypeStruct((n, D_MODEL), F32),
        compiler_params=_params(("parallel",)),
        name="mla_out",
    )(o_lat, wuv, wo)


def _split_dot(x, w_bf, pieces):
    out = None
    rem = x
    for _ in range(pieces):
        hi = rem.astype(BF16)
        part = _dot(hi, w_bf)
        out = part if out is None else out + part
        rem = rem - hi.astype(F32)
    return out


def _split_dot_lhs(tri_bf, x):
    out = None
    rem = x
    for _ in range(3):
        hi = rem.astype(BF16)
        part = _dot(tri_bf, hi)
        out = part if out is None else out + part
        rem = rem - hi.astype(F32)
    return out


def _expand_heads(x, hg):
    lane_head = lax.broadcasted_iota(jnp.int32, x.shape, 1) // RW_HEAD
    zero = jnp.zeros((), x.dtype)
    return jnp.concatenate([jnp.where(lane_head == hh, x, zero) for hh in range(hg)], axis=0)


def _fold_heads(x, hg):
    c = x.shape[0] // hg
    out = x[0:c]
    for hh in range(1, hg):
        out = out + x[hh * c:(hh + 1) * c]
    return out


def _rwkv_kernel(u_ref, sh_ref, s0_ref, mu_ref, w0_ref, w2a2_ref, a0_ref, g2_ref, kkw_ref, ka_ref, rk_ref,
                 lnw_ref, lnb_ref, ones_ref, wo_ref,
                 ob_ref, sout_ref,
                 state_ref, prev_ref, p_ref, mk_ref, *, c, hg, nseq):
    t = pl.program_id(1)
    tt = u_ref.shape[1]
    nc = tt // c
    ng = RW_HEADS // hg
    w = hg * RW_HEAD
    r_ = hg * c

    blockmask = (lax.broadcasted_iota(jnp.int32, (w, w), 0) // RW_HEAD
                 == lax.broadcasted_iota(jnp.int32, (w, w), 1) // RW_HEAD)

    @pl.when(t == 0)
    def _():
        prev_ref[...] = sh_ref[...]
        for s in range(nseq):
            for g in range(ng):
                s0g = s0_ref[s, :, g * w:(g + 1) * w]
                state_ref[s, g] = jnp.where(blockmask, jnp.concatenate([s0g] * hg, axis=0), 0.0)

    u = u_ref[...].reshape(nseq * tt, RW_COLS)
    row = lax.broadcasted_iota(jnp.int32, u.shape, 0)
    u_prev = pltpu.roll(u, 1, 0)
    for s in range(nseq):
        u_prev = jnp.where(row == s * tt, prev_ref[s], u_prev)
        prev_ref[s] = u[(s + 1) * tt - 1:(s + 1) * tt, :]
    us = u + (u_prev - u) * mu_ref[...]
    r = us[:, 0:RW_DIM]
    k = us[:, RW_DIM:2 * RW_DIM]
    v = us[:, 2 * RW_DIM:3 * RW_DIM]
    o1 = 3 * RW_DIM
    wa = us[:, o1:o1 + DECAY_LORA + AAA_LORA]
    gd = us[:, o1 + DECAY_LORA + AAA_LORA:]
    lane = lax.broadcasted_iota(jnp.int32, wa.shape, 1)
    wa = jnp.where(lane < DECAY_LORA, jnp.tanh(wa), wa)
    lo = _dot(wa.astype(BF16), w2a2_ref[...])
    w_raw = -jax.nn.softplus(-(w0_ref[...] + lo[:, :RW_DIM])) - 0.5
    logw = -jnp.exp(w_raw)
    a = jax.nn.sigmoid(a0_ref[...] + lo[:, RW_DIM:])
    g = _dot(jax.nn.sigmoid(gd).astype(BF16), g2_ref[...])
    ones_bd = ones_ref[...]
    kk = k * kkw_ref[...]
    kk = kk / jnp.maximum(jnp.sqrt(_split_dot(kk * kk, ones_bd, 1)), 1e-12)
    k2 = k * (1.0 + (a - 1.0) * ka_ref[...])

    tri = (lax.broadcasted_iota(jnp.int32, (c, c), 1) <= lax.broadcasted_iota(jnp.int32, (c, c), 0)).astype(BF16)
    rr = lax.broadcasted_iota(jnp.int32, (r_, r_), 0)
    cc = lax.broadcasted_iota(jnp.int32, (r_, r_), 1)
    strict_bd = (rr // c == cc // c) & (cc % c < rr % c)
    eye = (rr == cc).astype(F32)
    incl_wide = (lax.broadcasted_iota(jnp.int32, (c, r_), 1) % c
                 <= lax.broadcasted_iota(jnp.int32, (c, r_), 0))

    inst = [(s, ci, g_) for s in range(nseq) for ci in range(nc) for g_ in range(ng)]
    number = {key: idx for idx, key in enumerate(inst)}
    pre = {}
    for s in range(nseq):
        for ci in range(nc):
            rows = slice(s * tt + ci * c, s * tt + (ci + 1) * c)
            lw = logw[rows]
            cs = _split_dot_lhs(tri, lw)
            e_cur = jnp.exp(cs)
            e_inv = jnp.exp(-cs)
            kkc = kk[rows]
            at = (-kkc * jnp.exp(cs - lw)).astype(BF16)
            bt = (kkc * a[rows] * e_inv).astype(BF16)
            kt = (k2[rows] * e_inv).astype(BF16)
            rt = (r[rows] * e_cur).astype(BF16)
            vb = v[rows].astype(BF16)
            for g_ in range(ng):
                sl = slice(g_ * w, (g_ + 1) * w)
                pre[s, ci, g_] = dict(a=at[:, sl], b=bt[:, sl], k=kt[:, sl], r=rt[:, sl], v=vb[:, sl],
                                      p_end=e_cur[c - 1:c, sl])
    for key in inst:
        z = pre[key]
        ax, bx, kx = (_expand_heads(z[name], hg) for name in ("a", "b", "k"))
        z["n1"] = _dot_nt(jnp.concatenate([ax, z["r"]], axis=0), jnp.concatenate([bx, kx], axis=0))
    for key in inst:
        z = pre[key]
        n1 = z.pop("n1")
        m_ab = jnp.where(strict_bd, n1[:r_, :r_], 0.0)
        p_ref[number[key]] = eye + m_ab
        mk_ref[number[key]] = m_ab.astype(BF16)
        z["m_ak"] = jnp.where(strict_bd, n1[:r_, r_:], 0.0).astype(BF16)
        z["a_rb"] = jnp.where(incl_wide, n1[r_:, :r_], 0.0).astype(BF16)
        z["a_rk"] = jnp.where(incl_wide, n1[r_:, r_:], 0.0).astype(BF16)
    for key in inst:
        z = pre[key]
        vx = _expand_heads(z["v"], hg)
        z["wv"] = _dot(z.pop("m_ak"), vx)
        z["yv"] = _dot(z.pop("a_rk"), vx)

    def inverse_level(_, carry):
        squares = []
        for idx in range(len(inst)):
            mk = mk_ref[idx]
            squares.append(_dot(mk, mk).astype(BF16))
        for idx, mk2 in enumerate(squares):
            p = p_ref[idx]
            p_ref[idx] = p + _dot(p.astype(BF16), mk2)
            mk_ref[idx] = mk2
        return carry

    levels = max(c.bit_length() - 2, 0)
    lax.fori_loop(0, levels, inverse_level, 0)

    y_part = {}
    for ci in range(nc):
        keys = [(s, ci, g_) for s in range(nseq) for g_ in range(ng)]
        state = {key: state_ref[key[0], key[2]] for key in keys}
        sa = {key: _dot_nt(jnp.concatenate([pre[key]["a"], pre[key]["r"]], axis=0), state[key].astype(BF16))
              for key in keys}
        ux = {key: _dot(p_ref[number[key]].astype(BF16),
                        (_expand_heads(sa[key][:c], hg) + pre[key]["wv"]).astype(BF16))
              for key in keys}
        for key in keys:
            z = pre[key]
            y_part[key] = sa[key][c:] + _dot(z["a_rb"], ux[key].astype(BF16)) + z["yv"]
        ds = {key: _dot_tn(jnp.concatenate([_fold_heads(ux[key], hg).astype(BF16), pre[key]["v"]], axis=0),
                           jnp.concatenate([pre[key]["b"], pre[key]["k"]], axis=0))
              for key in keys}
        for key in keys:
            state_ref[key[0], key[2]] = (state[key] + jnp.where(blockmask, ds[key], 0.0)) * pre[key]["p_end"]
    y_rows = []
    for s in range(nseq):
        for ci in range(nc):
            ys = [y_part[s, ci, g_] for g_ in range(ng)]
            y_rows.append(ys[0] if ng == 1 else jnp.concatenate(ys, axis=1))
    y = y_rows[0] if len(y_rows) == 1 else jnp.concatenate(y_rows, axis=0)

    inv_n = 1.0 / RW_HEAD
    mean = _split_dot(y, ones_bd, 2) * inv_n
    d = y - mean
    var = _split_dot(d * d, ones_bd, 1) * inv_n
    yn = d * lax.rsqrt(var + GN_EPS) * lnw_ref[...] + lnb_ref[...]
    bonus = _split_dot(r * k2 * rk_ref[...], ones_bd, 1) * v
    ob_ref[...] = _dot(((yn + bonus) * g).astype(BF16), wo_ref[...]).reshape(nseq, tt, D_MODEL)

    @pl.when(t == pl.num_programs(1) - 1)
    def _():
        for s in range(nseq):
            for g_ in range(ng):
                sout_ref[s, :, g_ * w:(g_ + 1) * w] = _fold_heads(state_ref[s, g_], hg)


RW_STEP_TOKENS = 512


def _rwkv(u_rw, shift_in, s0, w, bsz, t):
    if t >= 64:
        c, hg, nseq = 64, 4, _pick_tile(bsz, 8)
        tt = _pick_tile(t, max(c, RW_STEP_TOKENS // nseq))
    else:
        c, hg, nseq = t, RW_HEADS, _pick_tile(bsz, 8)
        tt = t
    nt = t // tt
    ng = RW_HEADS // hg
    wd_ = hg * RW_HEAD
    n_inst = nseq * (tt // c) * ng
    kern = functools.partial(_rwkv_kernel, c=c, hg=hg, nseq=nseq)
    vec = lambda n: _const_spec((1, n))
    o_b, s_new = pl.pallas_call(
        kern,
        grid=(bsz // nseq, nt),
        in_specs=[pl.BlockSpec((nseq, tt, RW_COLS), lambda b, i: (b, i, 0)),
                  pl.BlockSpec((nseq, 1, RW_COLS), lambda b, i: (b, 0, 0)),
                  pl.BlockSpec((nseq, RW_HEAD, RW_DIM), lambda b, i: (b, 0, 0)),
                  vec(RW_COLS), vec(RW_DIM), _const_spec((DECAY_LORA + AAA_LORA, 2 * RW_DIM)), vec(RW_DIM),
                  _const_spec((GATE_LORA, RW_DIM)), vec(RW_DIM), vec(RW_DIM), vec(RW_DIM), vec(RW_DIM),
                  vec(RW_DIM), _const_spec((RW_DIM, RW_DIM)), _const_spec((RW_DIM, D_MODEL))],
        out_specs=[pl.BlockSpec((nseq, tt, D_MODEL), lambda b, i: (b, i, 0)),
                   pl.BlockSpec((nseq, RW_HEAD, RW_DIM), lambda b, i: (b, 0, 0))],
        out_shape=[jax.ShapeDtypeStruct((bsz, t, D_MODEL), F32),
                   jax.ShapeDtypeStruct((bsz, RW_HEAD, RW_DIM), F32)],
        scratch_shapes=[pltpu.VMEM((nseq, ng, wd_, wd_), F32), pltpu.VMEM((nseq, 1, RW_COLS), F32),
                        pltpu.VMEM((n_inst, hg * c, hg * c), F32), pltpu.VMEM((n_inst, hg * c, hg * c), BF16)],
        compiler_params=_params(("parallel", "arbitrary")),
        name="rwkv",
    )(u_rw.reshape(bsz, t, RW_COLS), shift_in, s0, w["rw_mu"], w["rw_w0"], w["rw_w2a2"], w["rw_a0"], w["rw_g2"],
      w["rw_kk"], w["rw_ka"], w["rw_rk"], w["rw_ln_w"], w["rw_ln_b"], w["ones_bd"], w["w_o_rw"])
    return o_b.reshape(bsz * t, D_MODEL), s_new


def _merge_kernel(x1_ref, oa_ref, ob_ref, gate_ref, wout_ref, n2_ref, wg_ref, wu_ref, wd_ref, nf_ref, y_ref):
    mixed = (gate_ref[:, :D_MODEL] * oa_ref[...] + gate_ref[:, D_MODEL:] * ob_ref[...]).astype(BF16)
    x2 = x1_ref[...] + _dot(mixed, wout_ref[...])
    hn = _rmsnorm(x2, n2_ref[...]).astype(BF16)
    x3 = x2 + 0.5 * _swiglu(hn, wg_ref, wu_ref, wd_ref)
    y_ref[...] = _rmsnorm(x3, nf_ref[...])


def _merge(x1, oa, ob, gate, wout, n2, wg, wu, wd, nf):
    n = x1.shape[0]
    tm = _pick_tile(n, 256)
    row = lambda w_: pl.BlockSpec((tm, w_), lambda i: (i, 0))
    return pl.pallas_call(
        _merge_kernel,
        grid=(n // tm,),
        in_specs=[row(D_MODEL), row(D_MODEL), row(D_MODEL), row(2 * D_MODEL),
                  _const_spec((D_MODEL, D_MODEL)), _const_spec((1, D_MODEL)), _const_spec((D_MODEL, D_FF)),
                  _const_spec((D_MODEL, D_FF)), _const_spec((D_FF, D_MODEL)), _const_spec((1, D_MODEL))],
        out_specs=row(D_MODEL),
        out_shape=jax.ShapeDtypeStruct((n, D_MODEL), F32),
        compiler_params=_params(("parallel",)),
        name="merge_ffn2",
    )(x1, oa, ob, gate, wout, n2, wg, wu, wd, nf)


def _rope_tables(pos):
    inv = 1.0 / (ROPE_THETA ** (jnp.arange(0, QK_ROPE, 2, dtype=F32) / QK_ROPE))
    ang = pos.astype(F32)[:, None] * inv[None, :]
    pad = jnp.zeros((pos.shape[0], ROPE_PAD - QK_ROPE), F32)
    cos, sin = jnp.cos(ang), jnp.sin(ang)
    return jnp.concatenate([cos, cos, pad], axis=1), jnp.concatenate([sin, sin, pad], axis=1)


def _rot_cols(wm):
    half = QK_ROPE // 2
    return jnp.concatenate([-wm[..., half:], wm[..., :half]], axis=-1)


def _pad_cols(wm, width):
    return jnp.pad(wm, [(0, 0)] * (wm.ndim - 1) + [(0, width - wm.shape[-1])])


def _prep_weights(p):
    w_in = p["w_in"]
    w_kpe = w_in[:, Q_LORA + KV_LORA:MLA_COLS]
    w_mla = jnp.concatenate([w_in[:, :Q_LORA + KV_LORA], _pad_cols(w_kpe, ROPE_PAD),
                             _pad_cols(_rot_cols(w_kpe), ROPE_PAD)], axis=1)
    w_qb = p["w_qb"].reshape(Q_LORA, MLA_HEADS, QK_NOPE + QK_ROPE)
    w_nope = w_qb[:, :, :QK_NOPE].reshape(Q_LORA, MLA_HEADS * QK_NOPE)
    w_pe = w_qb[:, :, QK_NOPE:]
    w_uk = jnp.transpose(p["w_uk"], (1, 2, 0))
    eye = jnp.eye(MLA_HEADS, dtype=F32)
    w_uk_bd = (eye[:, None, :, None] * w_uk[:, :, None, :]).reshape(MLA_HEADS * QK_NOPE, MLA_HEADS * KV_LORA)
    zeros = jnp.zeros((DECAY_LORA, RW_DIM), F32)
    w2a2 = jnp.concatenate([jnp.concatenate([p["rw_w2"], zeros], axis=1),
                            jnp.concatenate([zeros, p["rw_a2"]], axis=1)], axis=0)
    head = jnp.arange(RW_DIM) // RW_HEAD
    row = lambda v: v.reshape(1, -1)
    return {
        "w_mla": w_mla.astype(BF16),
        "w_rw": w_in[:, MLA_COLS:MLA_COLS + RW_COLS].astype(BF16),
        "w_gate": w_in[:, MLA_COLS + RW_COLS:].astype(BF16),
        "q_norm": row(p["q_norm"]), "kv_norm": row(p["kv_norm"]),
        "w_nope": w_nope.astype(BF16),
        "w_pe": _pad_cols(w_pe, ROPE_PAD).reshape(Q_LORA, MLA_HEADS * ROPE_PAD).astype(BF16),
        "w_pe_rot": _pad_cols(_rot_cols(w_pe), ROPE_PAD).reshape(Q_LORA, MLA_HEADS * ROPE_PAD).astype(BF16),
        "w_uk_bd": w_uk_bd.astype(BF16),
        "w_uv": jnp.transpose(p["w_uv"], (1, 0, 2)).astype(BF16),
        "w_o_mla": p["w_o_mla"].astype(BF16),
        "rw_mu": row(p["rw_mu"]), "rw_w0": row(p["rw_w0"]), "rw_w2a2": w2a2.astype(BF16),
        "rw_a0": row(p["rw_a0"]), "rw_g2": p["rw_g2"].astype(BF16), "rw_kk": row(p["rw_kk"]),
        "rw_ka": row(p["rw_ka"]), "rw_rk": row(p["rw_rk"]), "rw_ln_w": row(p["rw_ln_w"]),
        "rw_ln_b": row(p["rw_ln_b"]),
        "ones_bd": (head[:, None] == head[None, :]).astype(BF16),
        "w_o_rw": p["w_o_rw"].astype(BF16),
        "w_out": p["w_out"].astype(BF16),
        "ffn1_norm": row(p["ffn1_norm"]), "mix_norm": row(p["mix_norm"]), "ffn2_norm": row(p["ffn2_norm"]),
        "ffn1_wg": p["ffn1_wg"].astype(BF16), "ffn1_wu": p["ffn1_wu"].astype(BF16),
        "ffn1_wd": p["ffn1_wd"].astype(BF16),
        "ffn2_wg": p["ffn2_wg"].astype(BF16), "ffn2_wu": p["ffn2_wu"].astype(BF16),
        "ffn2_wd": p["ffn2_wd"].astype(BF16),
    }


def _state_to_lanes(s):
    b = s.shape[0]
    return jnp.transpose(s, (0, 2, 1, 3)).reshape(b, RW_HEAD, RW_DIM)


def _state_from_lanes(s):
    b = s.shape[0]
    return jnp.transpose(s.reshape(b, RW_HEAD, RW_HEADS, RW_HEAD), (0, 2, 1, 3))


def _layer(x, cos, sin, shift_in, s0, w, final_norm, attend):
    bsz, t, _ = x.shape
    n = bsz * t
    x1, h = _ffn1(x.reshape(n, D_MODEL), w["ffn1_norm"], w["ffn1_wg"], w["ffn1_wu"], w["ffn1_wd"], w["mix_norm"])
    q, kv, ckv, kpe, u_rw, gate = _proj(h, cos, sin, w)
    o_a = attend(q, kv)
    o_b, s_new = _rwkv(u_rw, shift_in.reshape(bsz, 1, RW_COLS), _state_to_lanes(s0), w, bsz, t)
    y = _merge(x1, o_a, o_b, gate, w["w_out"], w["ffn2_norm"], w["ffn2_wg"], w["ffn2_wu"], w["ffn2_wd"],
               final_norm.reshape(1, D_MODEL))
    return (y.reshape(bsz, t, D_MODEL), ckv.reshape(bsz, t, KV_LORA), kpe.reshape(bsz, t, QK_ROPE),
            _state_from_lanes(s_new), u_rw.reshape(bsz, t, RW_COLS)[:, -1])


def kernel(x_prompt, x_sample, cache_ckv, cache_kpe, state_wkv, state_shift, page_table, ffn1_norm, ffn1_wg,
           ffn1_wu, ffn1_wd, mix_norm, w_in, q_norm, kv_norm, w_qb, w_uk, w_uv, w_o_mla, rw_mu, rw_w0, rw_w2,
           rw_a0, rw_a2, rw_g2, rw_kk, rw_ka, rw_rk, rw_ln_w, rw_ln_b, w_o_rw, w_out, ffn2_norm, ffn2_wg,
           ffn2_wu, ffn2_wd, final_norm):
    depth = w_in.shape[0]
    assert depth == 1, "the final norm is fused into the layer's last stage"
    layer_params = dict(ffn1_norm=ffn1_norm, ffn1_wg=ffn1_wg, ffn1_wu=ffn1_wu, ffn1_wd=ffn1_wd, mix_norm=mix_norm,
                        w_in=w_in, q_norm=q_norm, kv_norm=kv_norm, w_qb=w_qb, w_uk=w_uk, w_uv=w_uv,
                        w_o_mla=w_o_mla, rw_mu=rw_mu, rw_w0=rw_w0, rw_w2=rw_w2, rw_a0=rw_a0, rw_a2=rw_a2,
                        rw_g2=rw_g2, rw_kk=rw_kk, rw_ka=rw_ka, rw_rk=rw_rk, rw_ln_w=rw_ln_w, rw_ln_b=rw_ln_b,
                        w_o_rw=w_o_rw, w_out=w_out, ffn2_norm=ffn2_norm, ffn2_wg=ffn2_wg, ffn2_wu=ffn2_wu,
                        ffn2_wd=ffn2_wd)
    w = _prep_weights({k: v[0] for k, v in layer_params.items()})

    bp, tp, _ = x_prompt.shape
    bs, ts, _ = x_sample.shape
    n_pages = page_table.shape[1]
    past_len = n_pages * cache_ckv.shape[2]

    cos_p, sin_p = _rope_tables(jnp.arange(tp, dtype=jnp.int32))
    attend_p = lambda q, kv: _attn_prompt(q, kv, w["w_uv"], w["w_o_mla"], bp, tp)
    out_p = _layer(x_prompt, cos_p, sin_p, jnp.zeros((bp, RW_COLS), F32),
                   jnp.zeros((bp, RW_HEADS, RW_HEAD, RW_HEAD), F32), w, final_norm, attend_p)

    n_s = bs * ts
    tile_s = _pick_tile(n_s, 512)
    pos_s = past_len + jnp.arange(tile_s, dtype=jnp.int32) % ts
    cos_s, sin_s = _rope_tables(pos_s)

    def attend_s(q, kv):
        q_b = jnp.transpose(q.reshape(MLA_HEADS, bs, ts, QK_WIDTH), (1, 0, 2, 3)).reshape(bs, MLA_HEADS * ts, QK_WIDTH)
        kv_new = kv.astype(F32).reshape(bs, ts, QK_WIDTH)
        o_lat = _attn_sample(page_table, q_b, kv_new, cache_ckv[0], jnp.swapaxes(cache_kpe[0], 1, 2))
        o_lat = jnp.transpose(o_lat.reshape(bs, MLA_HEADS, ts, KV_LORA), (1, 0, 2, 3)).reshape(MLA_HEADS, n_s, KV_LORA)
        return _mla_out_call(o_lat, w["w_uv"], w["w_o_mla"])

    out_s = _layer(x_sample, cos_s, sin_s, state_shift[0], state_wkv[0], w, final_norm, attend_s)

    y_p, ckv_p, kpe_p, wkv_p, sh_p = out_p
    y_s, ckv_s, kpe_s, wkv_s, sh_s = out_s
    stack = lambda z: z[None]
    return (y_p, y_s, stack(ckv_p), stack(kpe_p), stack(wkv_p), stack(sh_p),
            stack(ckv_s), stack(kpe_s), stack(wkv_s), stack(sh_s))
```

```python
import functools

import jax
import jax.numpy as jnp
from jax import lax
from jax.experimental import pallas as pl
from jax.experimental.pallas import tpu as pltpu

F32 = jnp.float32
BF16 = jnp.bfloat16

D_MODEL = 1024
MLA_HEADS = 8
QK_NOPE = 64
QK_ROPE = 32
V_DIM = 64
Q_LORA = 384
KV_LORA = 256
ROPE_THETA = 10000.0
SM_SCALE = (QK_NOPE + QK_ROPE) ** -0.5
RW_HEADS = 8
RW_HEAD = 64
RW_DIM = RW_HEADS * RW_HEAD
DECAY_LORA = 64
AAA_LORA = 64
GATE_LORA = 128
GN_EPS = RW_HEAD * 1e-5
D_FF = 2816
RMS_EPS = 1e-6
NEG_INF = -1e30
MLA_COLS = Q_LORA + KV_LORA + QK_ROPE
RW_COLS = 3 * RW_DIM + DECAY_LORA + AAA_LORA + GATE_LORA

LANES = 128
ROPE_PAD = LANES
QK_WIDTH = KV_LORA + ROPE_PAD
FF_CHUNKS = 2
VMEM_LIMIT = 56 * 1024 * 1024

_NT = (((1,), (1,)), ((), ()))
_TN = (((0,), (0,)), ((), ()))


def _pick_tile(n, pref):
    t = pref
    while n % t:
        t //= 2
    return t


def _const_spec(shape):
    nd = len(shape)
    return pl.BlockSpec(shape, lambda *_: (0,) * nd, pipeline_mode=pl.Buffered(1))


def _params(sem):
    return pltpu.CompilerParams(dimension_semantics=sem, vmem_limit_bytes=VMEM_LIMIT)


def _dot(a, b):
    return jnp.dot(a, b, preferred_element_type=F32)


def _dot_nt(a, b):
    return lax.dot_general(a, b, _NT, preferred_element_type=F32)


def _dot_tn(a, b):
    return lax.dot_general(a, b, _TN, preferred_element_type=F32)


def _rmsnorm(x, g):
    return x * lax.rsqrt(jnp.mean(x * x, axis=-1, keepdims=True) + RMS_EPS) * g


def _swiglu(hn, wg_ref, wu_ref, wd_ref):
    fc = D_FF // FF_CHUNKS
    acc = None
    for c in range(FF_CHUNKS):
        g = _dot(hn, wg_ref[:, c * fc:(c + 1) * fc])
        u = _dot(hn, wu_ref[:, c * fc:(c + 1) * fc])
        act = (g * jax.nn.sigmoid(g) * u).astype(BF16)
        part = _dot(act, wd_ref[c * fc:(c + 1) * fc, :])
        acc = part if acc is None else acc + part
    return acc


def _ffn1_kernel(x_ref, n1_ref, wg_ref, wu_ref, wd_ref, nm_ref, x1_ref, h_ref):
    x = x_ref[...]
    hn = _rmsnorm(x, n1_ref[...]).astype(BF16)
    x1 = x + 0.5 * _swiglu(hn, wg_ref, wu_ref, wd_ref)
    x1_ref[...] = x1
    h_ref[...] = _rmsnorm(x1, nm_ref[...]).astype(BF16)


def _ffn1(x, n1, wg, wu, wd, nm):
    n = x.shape[0]
    tm = _pick_tile(n, 512)
    row = lambda w: pl.BlockSpec((tm, w), lambda i: (i, 0))
    return pl.pallas_call(
        _ffn1_kernel,
        grid=(n // tm,),
        in_specs=[row(D_MODEL), _const_spec((1, D_MODEL)), _const_spec((D_MODEL, D_FF)),
                  _const_spec((D_MODEL, D_FF)), _const_spec((D_FF, D_MODEL)), _const_spec((1, D_MODEL))],
        out_specs=[row(D_MODEL), row(D_MODEL)],
        out_shape=[jax.ShapeDtypeStruct((n, D_MODEL), F32), jax.ShapeDtypeStruct((n, D_MODEL), BF16)],
        compiler_params=_params(("parallel",)),
        name="ffn1",
    )(x, n1, wg, wu, wd, nm)


def _proj_kernel(h_ref, cos_ref, sin_ref, wmla_ref, wrw_ref, wgate_ref, qn_ref, kvn_ref,
                 wnope_ref, wpe_ref, wper_ref, wuk_ref,
                 q_ref, kv_ref, ckv_ref, kpe_ref, urw_ref, gate_ref):
    h = h_ref[...]
    cos = cos_ref[...]
    sin = sin_ref[...]
    u = _dot(h, wmla_ref[...])
    qa = u[:, :Q_LORA]
    c0 = Q_LORA + KV_LORA
    ckv = _rmsnorm(u[:, Q_LORA:c0], kvn_ref[...])
    kpe = u[:, c0:c0 + ROPE_PAD] * cos + u[:, c0 + ROPE_PAD:] * sin
    ckv_ref[...] = ckv
    kpe_ref[...] = kpe[:, :QK_ROPE]
    kv_ref[...] = jnp.concatenate([ckv, kpe], axis=1).astype(BF16)

    qn = _rmsnorm(qa, qn_ref[...]).astype(BF16)
    q_nope = _dot(qn, wnope_ref[...]).astype(BF16)
    pair_w = 2 * QK_NOPE
    q_lat = jnp.concatenate([_dot(q_nope[:, pp * pair_w:(pp + 1) * pair_w], wuk_ref[pp])
                             for pp in range(MLA_HEADS // 2)], axis=1)
    qp = _dot(qn, wpe_ref[...])
    qr = _dot(qn, wper_ref[...])
    for hh in range(MLA_HEADS):
        ql = q_lat[:, hh * KV_LORA:(hh + 1) * KV_LORA] * SM_SCALE
        sl = slice(hh * ROPE_PAD, (hh + 1) * ROPE_PAD)
        pe = (qp[:, sl] * cos + qr[:, sl] * sin) * SM_SCALE
        q_ref[hh] = jnp.concatenate([ql, pe], axis=1).astype(BF16)

    urw_ref[...] = _dot(h, wrw_ref[...])
    gate_ref[...] = jax.nn.sigmoid(_dot(h, wgate_ref[...]))


def _proj(h, cos, sin, w):
    n = h.shape[0]
    tm = _pick_tile(cos.shape[0], _pick_tile(n, 512))
    nrb = cos.shape[0] // tm
    row = lambda wd_: pl.BlockSpec((tm, wd_), lambda i: (i, 0))
    rope = pl.BlockSpec((tm, ROPE_PAD), lambda i: (i % nrb, 0))
    wmla_w = Q_LORA + KV_LORA + 2 * ROPE_PAD
    return pl.pallas_call(
        _proj_kernel,
        grid=(n // tm,),
        in_specs=[row(D_MODEL), rope, rope,
                  _const_spec((D_MODEL, wmla_w)), _const_spec((D_MODEL, RW_COLS)),
                  _const_spec((D_MODEL, 2 * D_MODEL)), _const_spec((1, Q_LORA)), _const_spec((1, KV_LORA)),
                  _const_spec((Q_LORA, MLA_HEADS * QK_NOPE)), _const_spec((Q_LORA, MLA_HEADS * ROPE_PAD)),
                  _const_spec((Q_LORA, MLA_HEADS * ROPE_PAD)),
                  _const_spec((MLA_HEADS // 2, 2 * QK_NOPE, 2 * KV_LORA))],
        out_specs=[pl.BlockSpec((MLA_HEADS, tm, QK_WIDTH), lambda i: (0, i, 0)),
                   row(QK_WIDTH), row(KV_LORA), row(QK_ROPE), row(RW_COLS), row(2 * D_MODEL)],
        out_shape=[jax.ShapeDtypeStruct((MLA_HEADS, n, QK_WIDTH), BF16),
                   jax.ShapeDtypeStruct((n, QK_WIDTH), BF16),
                   jax.ShapeDtypeStruct((n, KV_LORA), F32),
                   jax.ShapeDtypeStruct((n, QK_ROPE), F32),
                   jax.ShapeDtypeStruct((n, RW_COLS), F32),
                   jax.ShapeDtypeStruct((n, 2 * D_MODEL), F32)],
        compiler_params=_params(("parallel",)),
        name="proj",
    )(h, cos, sin, w["w_mla"], w["w_rw"], w["w_gate"], w["q_norm"], w["kv_norm"],
      w["w_nope"], w["w_pe"], w["w_pe_rot"], w["w_uk_bd"])


def _lane_tile(x, n):
    if n <= LANES:
        return x[:, :n]
    return jnp.concatenate([x] * (n // LANES), axis=1)


def _softmax_update(s, v, m_ref, l_ref, acc_ref, rows):
    m_prev = m_ref[rows, :]
    m_new = jnp.maximum(m_prev, jnp.max(s, axis=-1, keepdims=True))
    alpha = jnp.exp(m_prev - m_new)
    p = jnp.exp(s - _lane_tile(m_new, s.shape[1]))
    l_ref[rows, :] = alpha * l_ref[rows, :] + jnp.sum(p, axis=-1, keepdims=True)
    pb = p.astype(BF16)
    if isinstance(v, (list, tuple)):
        off, pv = 0, None
        for piece in v:
            part = _dot(pb[:, off:off + piece.shape[0]], piece)
            pv = part if pv is None else pv + part
            off += piece.shape[0]
    else:
        pv = _dot(pb, v)
    acc_ref[rows, :] = _lane_tile(alpha, KV_LORA) * acc_ref[rows, :] + pv
    m_ref[rows, :] = m_new


def _softmax_update_groups(scores, v, m_ref, l_ref, acc_ref, row_groups):
    m_prev = [m_ref[rows, :] for rows in row_groups]
    m_new = [jnp.maximum(mp, jnp.max(s, axis=-1, keepdims=True)) for mp, s in zip(m_prev, scores)]
    alpha = [jnp.exp(mp - mn) for mp, mn in zip(m_prev, m_new)]
    p = [jnp.exp(s - _lane_tile(mn, s.shape[1])) for s, mn in zip(scores, m_new)]
    for rows, al, pg in zip(row_groups, alpha, p):
        l_ref[rows, :] = al * l_ref[rows, :] + jnp.sum(pg, axis=-1, keepdims=True)
    pv = [_dot(pg.astype(BF16), v) for pg in p]
    for rows, al, pvg, mn in zip(row_groups, alpha, pv, m_new):
        acc_ref[rows, :] = _lane_tile(al, KV_LORA) * acc_ref[rows, :] + pvg
        m_ref[rows, :] = mn


def _softmax_init(m_ref, l_ref, acc_ref):
    m_ref[...] = jnp.full(m_ref.shape, NEG_INF, F32)
    l_ref[...] = jnp.zeros(l_ref.shape, F32)
    acc_ref[...] = jnp.zeros(acc_ref.shape, F32)


def _softmax_result(l_ref, acc_ref):
    return acc_ref[...] / _lane_tile(l_ref[...], KV_LORA)


def _mla_out(o_lat, wuv_ref, wo_ref, rows):
    heads = [_dot(o_lat[hh * rows:(hh + 1) * rows].astype(BF16), wuv_ref[hh]).astype(BF16)
             for hh in range(MLA_HEADS)]
    return _dot(jnp.concatenate(heads, axis=1), wo_ref[...])


ATTN_SPLIT = 2


def _attn_prompt_kernel(q_ref, kv_ref, wuv_ref, wo_ref, o_ref, m_ref, l_ref, acc_ref, *, tq, tk):
    i = pl.program_id(1)
    hs = MLA_HEADS // ATTN_SPLIT
    sub = hs * tq
    _softmax_init(m_ref, l_ref, acc_ref)
    n_full = (i * tq) // tk

    def chunks(starts, diagonal_last):
        kvs = [kv_ref[pl.ds(k0, tk), :] for k0 in starts]
        if diagonal_last:
            delta = (lax.broadcasted_iota(jnp.int32, (sub, tk), 1)
                     - lax.broadcasted_iota(jnp.int32, (sub, tk), 0) % tq)
            visible = delta <= i * tq - starts[-1]
        row_groups = [pl.ds(g * sub, sub) for g in range(ATTN_SPLIT)]
        qs = [q_ref[g * hs:(g + 1) * hs].reshape(sub, QK_WIDTH) for g in range(ATTN_SPLIT)]
        scores = [[_dot_nt(q, kv) for q in qs] for kv in kvs]
        if diagonal_last:
            scores[-1] = [jnp.where(visible, s, NEG_INF) for s in scores[-1]]
        for kv, s_groups in zip(kvs, scores):
            _softmax_update_groups(s_groups, kv[:, :KV_LORA], m_ref, l_ref, acc_ref, row_groups)

    def pair_step(j, carry):
        k0 = pl.multiple_of(2 * j * tk, 2 * tk)
        chunks([k0, pl.multiple_of(k0 + tk, tk)], False)
        return carry

    lax.fori_loop(0, n_full // 2, pair_step, 0)
    k_diag = pl.multiple_of(n_full * tk, tk)

    @pl.when(n_full % 2 == 1)
    def _():
        chunks([pl.multiple_of(k_diag - tk, tk), k_diag], True)

    @pl.when(n_full % 2 == 0)
    def _():
        chunks([k_diag], True)

    o_ref[...] = _mla_out(_softmax_result(l_ref, acc_ref), wuv_ref, wo_ref, tq)


def _attn_prompt(q, kv, wuv, wo, bsz, t):
    tq = _pick_tile(t, 256)
    tk = _pick_tile(t, 512)
    nq = t // tq
    rows = MLA_HEADS * tq
    kern = functools.partial(_attn_prompt_kernel, tq=tq, tk=tk)
    return pl.pallas_call(
        kern,
        grid=(bsz, nq),
        in_specs=[pl.BlockSpec((MLA_HEADS, tq, QK_WIDTH), lambda b, i: (0, b * nq + i, 0)),
                  pl.BlockSpec((t, QK_WIDTH), lambda b, i: (b, 0)),
                  _const_spec((MLA_HEADS, KV_LORA, V_DIM)), _const_spec((MLA_HEADS * V_DIM, D_MODEL))],
        out_specs=pl.BlockSpec((tq, D_MODEL), lambda b, i: (b * nq + i, 0)),
        out_shape=jax.ShapeDtypeStruct((bsz * t, D_MODEL), F32),
        scratch_shapes=[pltpu.VMEM((rows, LANES), F32), pltpu.VMEM((rows, LANES), F32),
                        pltpu.VMEM((rows, KV_LORA), F32)],
        compiler_params=_params(("parallel", "arbitrary")),
        name="attn_prompt",
    )(q, kv, wuv, wo)


SAMPLE_SLOTS = 4
SAMPLE_PARTS = 2

def _attn_sample_kernel(pt_ref, q_ref, kvn_ref, cc_hbm, ck_hbm, o_ref,
                        cbuf, kbuf, csem, ksem, m_ref, l_ref, acc_ref, *, pgc, n_pages, tnew):
    b = pl.program_id(0)
    nch = n_pages // pgc
    ahead = SAMPLE_SLOTS - 2
    page_rows = cbuf.shape[2]

    def chunk_copies(seq, ch):
        slot = ch % SAMPLE_SLOTS
        out = []
        for p in range(pgc):
            page = pt_ref[seq * n_pages + ch * pgc + p]
            out.append(pltpu.make_async_copy(cc_hbm.at[page], cbuf.at[slot, p], csem.at[slot]))
            out.append(pltpu.make_async_copy(ck_hbm.at[page], kbuf.at[slot, p], ksem.at[slot]))
        return out

    def start_chunk(ch):
        if ch < nch:
            for cp in chunk_copies(b, ch):
                cp.start()
        else:
            @pl.when(b + 1 < pl.num_programs(0))
            def _():
                for cp in chunk_copies(b + 1, ch - nch):
                    cp.start()

    @pl.when(b == 0)
    def _():
        for ch in range(ahead):
            for cp in chunk_copies(b, ch):
                cp.start()

    _softmax_init(m_ref, l_ref, acc_ref)
    q = q_ref[...]
    rows = q.shape[0]
    all_rows = pl.ds(0, rows)
    q_lat = q[:, :KV_LORA]
    q_pe = q[:, KV_LORA:KV_LORA + QK_ROPE]

    scored = None
    for ch in range(nch):
        start_chunk(ch + ahead)
        for cp in chunk_copies(b, ch):
            cp.wait()
        slot = ch % SAMPLE_SLOTS
        part = pgc // SAMPLE_PARTS
        cs = [cbuf[slot, h * part:(h + 1) * part].reshape(part * page_rows, KV_LORA).astype(BF16)
              for h in range(SAMPLE_PARTS)]
        kt = jnp.concatenate([kbuf[slot, p] for p in range(pgc)], axis=1).astype(BF16)
        s = jnp.concatenate([_dot_nt(q_lat, c) for c in cs], axis=1) + _dot(q_pe, kt)
        if scored is not None:
            _softmax_update(scored[0], scored[1], m_ref, l_ref, acc_ref, all_rows)
        scored = (s, cs)
    _softmax_update(scored[0], scored[1], m_ref, l_ref, acc_ref, all_rows)

    kvn = kvn_ref[...].astype(BF16)
    sn = _dot_nt(q, kvn)
    qt = lax.broadcasted_iota(jnp.int32, (rows, tnew), 0) % tnew
    kt_ = lax.broadcasted_iota(jnp.int32, (rows, tnew), 1)
    sn = jnp.where(kt_ <= qt, sn, NEG_INF)
    _softmax_update(sn, kvn[:, :KV_LORA], m_ref, l_ref, acc_ref, all_rows)
    o_ref[...] = _softmax_result(l_ref, acc_ref)


def _attn_sample(page_table, q, kvn, cache_c, cache_kt):
    nb, n_pages = page_table.shape
    tnew = kvn.shape[1]
    rows = q.shape[1]
    page = cache_c.shape[1]
    pgc = _pick_tile(n_pages, 32)
    assert (n_pages // pgc) % SAMPLE_SLOTS == 0, "ring slots are static per chunk, also across sequences"
    kern = functools.partial(_attn_sample_kernel, pgc=pgc, n_pages=n_pages, tnew=tnew)
    grid_spec = pltpu.PrefetchScalarGridSpec(
        num_scalar_prefetch=1,
        grid=(nb,),
        in_specs=[pl.BlockSpec((None, rows, QK_WIDTH), lambda b, pt: (b, 0, 0)),
                  pl.BlockSpec((None, tnew, QK_WIDTH), lambda b, pt: (b, 0, 0)),
                  pl.BlockSpec(memory_space=pl.ANY), pl.BlockSpec(memory_space=pl.ANY)],
        out_specs=pl.BlockSpec((None, rows, KV_LORA), lambda b, pt: (b, 0, 0)),
        scratch_shapes=[pltpu.VMEM((SAMPLE_SLOTS, pgc, page, KV_LORA), F32),
                        pltpu.VMEM((SAMPLE_SLOTS, pgc, QK_ROPE, page), F32),
                        pltpu.SemaphoreType.DMA((SAMPLE_SLOTS,)), pltpu.SemaphoreType.DMA((SAMPLE_SLOTS,)),
                        pltpu.VMEM((rows, LANES), F32), pltpu.VMEM((rows, LANES), F32),
                        pltpu.VMEM((rows, KV_LORA), F32)],
    )
    return pl.pallas_call(
        kern,
        grid_spec=grid_spec,
        out_shape=jax.ShapeDtypeStruct((nb, rows, KV_LORA), F32),
        compiler_params=_params(("arbitrary",)),
        name="attn_sample",
    )(page_table.reshape(-1), q, kvn, cache_c, cache_kt)


def _mla_out_kernel(ol_ref, wuv_ref, wo_ref, o_ref):
    tm = ol_ref.shape[1]
    o_ref[...] = _mla_out(ol_ref[...].reshape(MLA_HEADS * tm, KV_LORA), wuv_ref, wo_ref, tm)


def _mla_out_call(o_lat, wuv, wo):
    n = o_lat.shape[1]
    tm = _pick_tile(n, 256)
    return pl.pallas_call(
        _mla_out_kernel,
        grid=(n // tm,),
        in_specs=[pl.BlockSpec((MLA_HEADS, tm, KV_LORA), lambda i: (0, i, 0)),
                  _const_spec((MLA_HEADS, KV_LORA, V_DIM)), _const_spec((MLA_HEADS * V_DIM, D_MODEL))],
        out_specs=pl.BlockSpec((tm, D_MODEL), lambda i: (i, 0)),
        out_shape=jax.ShapeDtypeStruct((n, D_MODEL), F32),
        compiler_params=_params(("parallel",)),
        name="mla_out",
    )(o_lat, wuv, wo)


def _split_dot(x, w_bf, pieces):
    out = None
    rem = x
    for _ in range(pieces):
        hi = rem.astype(BF16)
        part = _dot(hi, w_bf)
        out = part if out is None else out + part
        rem = rem - hi.astype(F32)
    return out


def _split_dot_lhs(tri_bf, x):
    out = None
    rem = x
    for _ in range(3):
        hi = rem.astype(BF16)
        part = _dot(tri_bf, hi)
        out = part if out is None else out + part
        rem = rem - hi.astype(F32)
    return out


def _expand_heads(x, hg):
    lane_head = lax.broadcasted_iota(jnp.int32, x.shape, 1) // RW_HEAD
    zero = jnp.zeros((), x.dtype)
    return jnp.concatenate([jnp.where(lane_head == hh, x, zero) for hh in range(hg)], axis=0)


def _fold_heads(x, hg):
    c = x.shape[0] // hg
    out = x[0:c]
    for hh in range(1, hg):
        out = out + x[hh * c:(hh + 1) * c]
    return out


def _rwkv_kernel(u_ref, sh_ref, s0_ref, mu_ref, w0_ref, w2a2_ref, a0_ref, g2_ref, kkw_ref, ka_ref, rk_ref,
                 lnw_ref, lnb_ref, ones_ref, wo_ref,
                 ob_ref, sout_ref,
                 state_ref, prev_ref, p_ref, mk_ref, *, c, hg, nseq):
    t = pl.program_id(1)
    tt = u_ref.shape[1]
    nc = tt // c
    ng = RW_HEADS // hg
    w = hg * RW_HEAD
    r_ = hg * c

    blockmask = (lax.broadcasted_iota(jnp.int32, (w, w), 0) // RW_HEAD
                 == lax.broadcasted_iota(jnp.int32, (w, w), 1) // RW_HEAD)

    @pl.when(t == 0)
    def _():
        prev_ref[...] = sh_ref[...]
        for s in range(nseq):
            for g in range(ng):
                s0g = s0_ref[s, :, g * w:(g + 1) * w]
                state_ref[s, g] = jnp.where(blockmask, jnp.concatenate([s0g] * hg, axis=0), 0.0)

    u = u_ref[...].reshape(nseq * tt, RW_COLS)
    row = lax.broadcasted_iota(jnp.int32, u.shape, 0)
    u_prev = pltpu.roll(u, 1, 0)
    for s in range(nseq):
        u_prev = jnp.where(row == s * tt, prev_ref[s], u_prev)
        prev_ref[s] = u[(s + 1) * tt - 1:(s + 1) * tt, :]
    us = u + (u_prev - u) * mu_ref[...]
    r = us[:, 0:RW_DIM]
    k = us[:, RW_DIM:2 * RW_DIM]
    v = us[:, 2 * RW_DIM:3 * RW_DIM]
    o1 = 3 * RW_DIM
    wa = us[:, o1:o1 + DECAY_LORA + AAA_LORA]
    gd = us[:, o1 + DECAY_LORA + AAA_LORA:]
    lane = lax.broadcasted_iota(jnp.int32, wa.shape, 1)
    wa = jnp.where(lane < DECAY_LORA, jnp.tanh(wa), wa)
    lo = _dot(wa.astype(BF16), w2a2_ref[...])
    w_raw = -jax.nn.softplus(-(w0_ref[...] + lo[:, :RW_DIM])) - 0.5
    logw = -jnp.exp(w_raw)
    a = jax.nn.sigmoid(a0_ref[...] + lo[:, RW_DIM:])
    g = _dot(jax.nn.sigmoid(gd).astype(BF16), g2_ref[...])
    ones_bd = ones_ref[...]
    kk = k * kkw_ref[...]
    kk = kk / jnp.maximum(jnp.sqrt(_split_dot(kk * kk, ones_bd, 1)), 1e-12)
    k2 = k * (1.0 + (a - 1.0) * ka_ref[...])

    tri = (lax.broadcasted_iota(jnp.int32, (c, c), 1) <= lax.broadcasted_iota(jnp.int32, (c, c), 0)).astype(BF16)
    rr = lax.broadcasted_iota(jnp.int32, (r_, r_), 0)
    cc = lax.broadcasted_iota(jnp.int32, (r_, r_), 1)
    strict_bd = (rr // c == cc // c) & (cc % c < rr % c)
    eye = (rr == cc).astype(F32)
    incl_wide = (lax.broadcasted_iota(jnp.int32, (c, r_), 1) % c
                 <= lax.broadcasted_iota(jnp.int32, (c, r_), 0))

    inst = [(s, ci, g_) for s in range(nseq) for ci in range(nc) for g_ in range(ng)]
    number = {key: idx for idx, key in enumerate(inst)}
    pre = {}
    for s in range(nseq):
        for ci in range(nc):
            rows = slice(s * tt + ci * c, s * tt + (ci + 1) * c)
            lw = logw[rows]
            cs = _split_dot_lhs(tri, lw)
            e_cur = jnp.exp(cs)
            e_inv = jnp.exp(-cs)
            kkc = kk[rows]
            at = (-kkc * jnp.exp(cs - lw)).astype(BF16)
            bt = (kkc * a[rows] * e_inv).astype(BF16)
            kt = (k2[rows] * e_inv).astype(BF16)
            rt = (r[rows] * e_cur).astype(BF16)
            vb = v[rows].astype(BF16)
            for g_ in range(ng):
                sl = slice(g_ * w, (g_ + 1) * w)
                pre[s, ci, g_] = dict(a=at[:, sl], b=bt[:, sl], k=kt[:, sl], r=rt[:, sl], v=vb[:, sl],
                                      p_end=e_cur[c - 1:c, sl])
    for key in inst:
        z = pre[key]
        ax, bx, kx = (_expand_heads(z[name], hg) for name in ("a", "b", "k"))
        z["n1"] = _dot_nt(jnp.concatenate([ax, z["r"]], axis=0), jnp.concatenate([bx, kx], axis=0))
    for key in inst:
        z = pre[key]
        n1 = z.pop("n1")
        m_ab = jnp.where(strict_bd, n1[:r_, :r_], 0.0)
        p_ref[number[key]] = eye + m_ab
        mk_ref[number[key]] = m_ab.astype(BF16)
        z["m_ak"] = jnp.where(strict_bd, n1[:r_, r_:], 0.0).astype(BF16)
        z["a_rb"] = jnp.where(incl_wide, n1[r_:, :r_], 0.0).astype(BF16)
        z["a_rk"] = jnp.where(incl_wide, n1[r_:, r_:], 0.0).astype(BF16)
    for key in inst:
        z = pre[key]
        vx = _expand_heads(z["v"], hg)
        z["wv"] = _dot(z.pop("m_ak"), vx)
        z["yv"] = _dot(z.pop("a_rk"), vx)

    def inverse_level(_, carry):
        squares = []
        for idx in range(len(inst)):
            mk = mk_ref[idx]
            squares.append(_dot(mk, mk).astype(BF16))
        for idx, mk2 in enumerate(squares):
            p = p_ref[idx]
            p_ref[idx] = p + _dot(p.astype(BF16), mk2)
            mk_ref[idx] = mk2
        return carry

    levels = max(c.bit_length() - 2, 0)
    lax.fori_loop(0, levels, inverse_level, 0)

    y_part = {}
    for ci in range(nc):
        keys = [(s, ci, g_) for s in range(nseq) for g_ in range(ng)]
        state = {key: state_ref[key[0], key[2]] for key in keys}
        sa = {key: _dot_nt(jnp.concatenate([pre[key]["a"], pre[key]["r"]], axis=0), state[key].astype(BF16))
              for key in keys}
        ux = {key: _dot(p_ref[number[key]].astype(BF16),
                        (_expand_heads(sa[key][:c], hg) + pre[key]["wv"]).astype(BF16))
              for key in keys}
        for key in keys:
            z = pre[key]
            y_part[key] = sa[key][c:] + _dot(z["a_rb"], ux[key].astype(BF16)) + z["yv"]
        ds = {key: _dot_tn(jnp.concatenate([_fold_heads(ux[key], hg).astype(BF16), pre[key]["v"]], axis=0),
                           jnp.concatenate([pre[key]["b"], pre[key]["k"]], axis=0))
              for key in keys}
        for key in keys:
            state_ref[key[0], key[2]] = (state[key] + jnp.where(blockmask, ds[key], 0.0)) * pre[key]["p_end"]
    y_rows = []
    for s in range(nseq):
        for ci in range(nc):
            ys = [y_part[s, ci, g_] for g_ in range(ng)]
            y_rows.append(ys[0] if ng == 1 else jnp.concatenate(ys, axis=1))
    y = y_rows[0] if len(y_rows) == 1 else jnp.concatenate(y_rows, axis=0)

    inv_n = 1.0 / RW_HEAD
    mean = _split_dot(y, ones_bd, 2) * inv_n
    d = y - mean
    var = _split_dot(d * d, ones_bd, 1) * inv_n
    yn = d * lax.rsqrt(var + GN_EPS) * lnw_ref[...] + lnb_ref[...]
    bonus = _split_dot(r * k2 * rk_ref[...], ones_bd, 1) * v
    ob_ref[...] = _dot(((yn + bonus) * g).astype(BF16), wo_ref[...]).reshape(nseq, tt, D_MODEL)

    @pl.when(t == pl.num_programs(1) - 1)
    def _():
        for s in range(nseq):
            for g_ in range(ng):
                sout_ref[s, :, g_ * w:(g_ + 1) * w] = _fold_heads(state_ref[s, g_], hg)


RW_STEP_TOKENS = 512


def _rwkv(u_rw, shift_in, s0, w, bsz, t):
    if t >= 64:
        c, hg, nseq = 64, 4, _pick_tile(bsz, 8)
        tt = _pick_tile(t, max(c, RW_STEP_TOKENS // nseq))
    else:
        c, hg, nseq = t, RW_HEADS, _pick_tile(bsz, 8)
        tt = t
    nt = t // tt
    ng = RW_HEADS // hg
    wd_ = hg * RW_HEAD
    n_inst = nseq * (tt // c) * ng
    kern = functools.partial(_rwkv_kernel, c=c, hg=hg, nseq=nseq)
    vec = lambda n: _const_spec((1, n))
    o_b, s_new = pl.pallas_call(
        kern,
        grid=(bsz // nseq, nt),
        in_specs=[pl.BlockSpec((nseq, tt, RW_COLS), lambda b, i: (b, i, 0)),
                  pl.BlockSpec((nseq, 1, RW_COLS), lambda b, i: (b, 0, 0)),
                  pl.BlockSpec((nseq, RW_HEAD, RW_DIM), lambda b, i: (b, 0, 0)),
                  vec(RW_COLS), vec(RW_DIM), _const_spec((DECAY_LORA + AAA_LORA, 2 * RW_DIM)), vec(RW_DIM),
                  _const_spec((GATE_LORA, RW_DIM)), vec(RW_DIM), vec(RW_DIM), vec(RW_DIM), vec(RW_DIM),
                  vec(RW_DIM), _const_spec((RW_DIM, RW_DIM)), _const_spec((RW_DIM, D_MODEL))],
        out_specs=[pl.BlockSpec((nseq, tt, D_MODEL), lambda b, i: (b, i, 0)),
                   pl.BlockSpec((nseq, RW_HEAD, RW_DIM), lambda b, i: (b, 0, 0))],
        out_shape=[jax.ShapeDtypeStruct((bsz, t, D_MODEL), F32),
                   jax.ShapeDtypeStruct((bsz, RW_HEAD, RW_DIM), F32)],
        scratch_shapes=[pltpu.VMEM((nseq, ng, wd_, wd_), F32), pltpu.VMEM((nseq, 1, RW_COLS), F32),
                        pltpu.VMEM((n_inst, hg * c, hg * c), F32), pltpu.VMEM((n_inst, hg * c, hg * c), BF16)],
        compiler_params=_params(("parallel", "arbitrary")),
        name="rwkv",
    )(u_rw.reshape(bsz, t, RW_COLS), shift_in, s0, w["rw_mu"], w["rw_w0"], w["rw_w2a2"], w["rw_a0"], w["rw_g2"],
      w["rw_kk"], w["rw_ka"], w["rw_rk"], w["rw_ln_w"], w["rw_ln_b"], w["ones_bd"], w["w_o_rw"])
    return o_b.reshape(bsz * t, D_MODEL), s_new


def _merge_kernel(x1_ref, oa_ref, ob_ref, gate_ref, wout_ref, n2_ref, wg_ref, wu_ref, wd_ref, nf_ref, y_ref):
    mixed = (gate_ref[:, :D_MODEL] * oa_ref[...] + gate_ref[:, D_MODEL:] * ob_ref[...]).astype(BF16)
    x2 = x1_ref[...] + _dot(mixed, wout_ref[...])
    hn = _rmsnorm(x2, n2_ref[...]).astype(BF16)
    x3 = x2 + 0.5 * _swiglu(hn, wg_ref, wu_ref, wd_ref)
    y_ref[...] = _rmsnorm(x3, nf_ref[...])


def _merge(x1, oa, ob, gate, wout, n2, wg, wu, wd, nf):
    n = x1.shape[0]
    tm = _pick_tile(n, 256)
    row = lambda w_: pl.BlockSpec((tm, w_), lambda i: (i, 0))
    return pl.pallas_call(
        _merge_kernel,
        grid=(n // tm,),
        in_specs=[row(D_MODEL), row(D_MODEL), row(D_MODEL), row(2 * D_MODEL),
                  _const_spec((D_MODEL, D_MODEL)), _const_spec((1, D_MODEL)), _const_spec((D_MODEL, D_FF)),
                  _const_spec((D_MODEL, D_FF)), _const_spec((D_FF, D_MODEL)), _const_spec((1, D_MODEL))],
        out_specs=row(D_MODEL),
        out_shape=jax.ShapeDtypeStruct((n, D_MODEL), F32),
        compiler_params=_params(("parallel",)),
        name="merge_ffn2",
    )(x1, oa, ob, gate, wout, n2, wg, wu, wd, nf)


def _rope_tables(pos):
    inv = 1.0 / (ROPE_THETA ** (jnp.arange(0, QK_ROPE, 2, dtype=F32) / QK_ROPE))
    ang = pos.astype(F32)[:, None] * inv[None, :]
    pad = jnp.zeros((pos.shape[0], ROPE_PAD - QK_ROPE), F32)
    cos, sin = jnp.cos(ang), jnp.sin(ang)
    return jnp.concatenate([cos, cos, pad], axis=1), jnp.concatenate([sin, sin, pad], axis=1)


def _rot_cols(wm):
    half = QK_ROPE // 2
    return jnp.concatenate([-wm[..., half:], wm[..., :half]], axis=-1)


def _pad_cols(wm, width):
    return jnp.pad(wm, [(0, 0)] * (wm.ndim - 1) + [(0, width - wm.shape[-1])])


def _prep_weights(p):
    w_in = p["w_in"]
    w_kpe = w_in[:, Q_LORA + KV_LORA:MLA_COLS]
    w_mla = jnp.concatenate([w_in[:, :Q_LORA + KV_LORA], _pad_cols(w_kpe, ROPE_PAD),
                             _pad_cols(_rot_cols(w_kpe), ROPE_PAD)], axis=1)
    w_qb = p["w_qb"].reshape(Q_LORA, MLA_HEADS, QK_NOPE + QK_ROPE)
    w_nope = w_qb[:, :, :QK_NOPE].reshape(Q_LORA, MLA_HEADS * QK_NOPE)
    w_pe = w_qb[:, :, QK_NOPE:]
    w_uk = jnp.transpose(p["w_uk"], (1, 2, 0)).reshape(MLA_HEADS // 2, 2, QK_NOPE, KV_LORA)
    same_head = jnp.eye(2, dtype=bool)[None, :, None, :, None]
    w_uk_bd = jnp.where(same_head, w_uk[:, :, :, None, :], 0.0).reshape(MLA_HEADS // 2, 2 * QK_NOPE, 2 * KV_LORA)
    zeros = jnp.zeros((DECAY_LORA, RW_DIM), F32)
    w2a2 = jnp.concatenate([jnp.concatenate([p["rw_w2"], zeros], axis=1),
                            jnp.concatenate([zeros, p["rw_a2"]], axis=1)], axis=0)
    head = jnp.arange(RW_DIM) // RW_HEAD
    row = lambda v: v.reshape(1, -1)
    return {
        "w_mla": w_mla.astype(BF16),
        "w_rw": w_in[:, MLA_COLS:MLA_COLS + RW_COLS].astype(BF16),
        "w_gate": w_in[:, MLA_COLS + RW_COLS:].astype(BF16),
        "q_norm": row(p["q_norm"]), "kv_norm": row(p["kv_norm"]),
        "w_nope": w_nope.astype(BF16),
        "w_pe": _pad_cols(w_pe, ROPE_PAD).reshape(Q_LORA, MLA_HEADS * ROPE_PAD).astype(BF16),
        "w_pe_rot": _pad_cols(_rot_cols(w_pe), ROPE_PAD).reshape(Q_LORA, MLA_HEADS * ROPE_PAD).astype(BF16),
        "w_uk_bd": w_uk_bd.astype(BF16),
        "w_uv": jnp.transpose(p["w_uv"], (1, 0, 2)).astype(BF16),
        "w_o_mla": p["w_o_mla"].astype(BF16),
        "rw_mu": row(p["rw_mu"]), "rw_w0": row(p["rw_w0"]), "rw_w2a2": w2a2.astype(BF16),
        "rw_a0": row(p["rw_a0"]), "rw_g2": p["rw_g2"].astype(BF16), "rw_kk": row(p["rw_kk"]),
        "rw_ka": row(p["rw_ka"]), "rw_rk": row(p["rw_rk"]), "rw_ln_w": row(p["rw_ln_w"]),
        "rw_ln_b": row(p["rw_ln_b"]),
        "ones_bd": (head[:, None] == head[None, :]).astype(BF16),
        "w_o_rw": p["w_o_rw"].astype(BF16),
        "w_out": p["w_out"].astype(BF16),
        "ffn1_norm": row(p["ffn1_norm"]), "mix_norm": row(p["mix_norm"]), "ffn2_norm": row(p["ffn2_norm"]),
        "ffn1_wg": p["ffn1_wg"].astype(BF16), "ffn1_wu": p["ffn1_wu"].astype(BF16),
        "ffn1_wd": p["ffn1_wd"].astype(BF16),
        "ffn2_wg": p["ffn2_wg"].astype(BF16), "ffn2_wu": p["ffn2_wu"].astype(BF16),
        "ffn2_wd": p["ffn2_wd"].astype(BF16),
    }


def _state_to_lanes(s):
    b = s.shape[0]
    return jnp.transpose(s, (0, 2, 1, 3)).reshape(b, RW_HEAD, RW_DIM)


def _state_from_lanes(s):
    b = s.shape[0]
    return jnp.transpose(s.reshape(b, RW_HEAD, RW_HEADS, RW_HEAD), (0, 2, 1, 3))


def _layer(x, cos, sin, shift_in, s0, w, final_norm, attend):
    bsz, t, _ = x.shape
    n = bsz * t
    x1, h = _ffn1(x.reshape(n, D_MODEL), w["ffn1_norm"], w["ffn1_wg"], w["ffn1_wu"], w["ffn1_wd"], w["mix_norm"])
    q, kv, ckv, kpe, u_rw, gate = _proj(h, cos, sin, w)
    o_a = attend(q, kv)
    o_b, s_new = _rwkv(u_rw, shift_in.reshape(bsz, 1, RW_COLS), _state_to_lanes(s0), w, bsz, t)
    y = _merge(x1, o_a, o_b, gate, w["w_out"], w["ffn2_norm"], w["ffn2_wg"], w["ffn2_wu"], w["ffn2_wd"],
               final_norm.reshape(1, D_MODEL))
    return (y.reshape(bsz, t, D_MODEL), ckv.reshape(bsz, t, KV_LORA), kpe.reshape(bsz, t, QK_ROPE),
            _state_from_lanes(s_new), u_rw.reshape(bsz, t, RW_COLS)[:, -1])


def kernel(x_prompt, x_sample, cache_ckv, cache_kpe, state_wkv, state_shift, page_table, ffn1_norm, ffn1_wg,
           ffn1_wu, ffn1_wd, mix_norm, w_in, q_norm, kv_norm, w_qb, w_uk, w_uv, w_o_mla, rw_mu, rw_w0, rw_w2,
           rw_a0, rw_a2, rw_g2, rw_kk, rw_ka, rw_rk, rw_ln_w, rw_ln_b, w_o_rw, w_out, ffn2_norm, ffn2_wg,
           ffn2_wu, ffn2_wd, final_norm):
    depth = w_in.shape[0]
    assert depth == 1, "the final norm is fused into the layer's last stage"
    layer_params = dict(ffn1_norm=ffn1_norm, ffn1_wg=ffn1_wg, ffn1_wu=ffn1_wu, ffn1_wd=ffn1_wd, mix_norm=mix_norm,
                        w_in=w_in, q_norm=q_norm, kv_norm=kv_norm, w_qb=w_qb, w_uk=w_uk, w_uv=w_uv,
                        w_o_mla=w_o_mla, rw_mu=rw_mu, rw_w0=rw_w0, rw_w2=rw_w2, rw_a0=rw_a0, rw_a2=rw_a2,
                        rw_g2=rw_g2, rw_kk=rw_kk, rw_ka=rw_ka, rw_rk=rw_rk, rw_ln_w=rw_ln_w, rw_ln_b=rw_ln_b,
                        w_o_rw=w_o_rw, w_out=w_out, ffn2_norm=ffn2_norm, ffn2_wg=ffn2_wg, ffn2_wu=ffn2_wu,
                        ffn2_wd=ffn2_wd)
    w = _prep_weights({k: v[0] for k, v in layer_params.items()})

    bp, tp, _ = x_prompt.shape
    bs, ts, _ = x_sample.shape
    n_pages = page_table.shape[1]
    past_len = n_pages * cache_ckv.shape[2]

    cos_p, sin_p = _rope_tables(jnp.arange(tp, dtype=jnp.int32))
    attend_p = lambda q, kv: _attn_prompt(q, kv, w["w_uv"], w["w_o_mla"], bp, tp)
    out_p = _layer(x_prompt, cos_p, sin_p, jnp.zeros((bp, RW_COLS), F32),
                   jnp.zeros((bp, RW_HEADS, RW_HEAD, RW_HEAD), F32), w, final_norm, attend_p)

    n_s = bs * ts
    tile_s = _pick_tile(n_s, 512)
    pos_s = past_len + jnp.arange(tile_s, dtype=jnp.int32) % ts
    cos_s, sin_s = _rope_tables(pos_s)

    def attend_s(q, kv):
        q_b = jnp.transpose(q.reshape(MLA_HEADS, bs, ts, QK_WIDTH), (1, 0, 2, 3)).reshape(bs, MLA_HEADS * ts, QK_WIDTH)
        kv_new = kv.astype(F32).reshape(bs, ts, QK_WIDTH)
        o_lat = _attn_sample(page_table, q_b, kv_new, cache_ckv[0], jnp.swapaxes(cache_kpe[0], 1, 2))
        o_lat = jnp.transpose(o_lat.reshape(bs, MLA_HEADS, ts, KV_LORA), (1, 0, 2, 3)).reshape(MLA_HEADS, n_s, KV_LORA)
        return _mla_out_call(o_lat, w["w_uv"], w["w_o_mla"])

    out_s = _layer(x_sample, cos_s, sin_s, state_shift[0], state_wkv[0], w, final_norm, attend_s)

    y_p, ckv_p, kpe_p, wkv_p, sh_p = out_p
    y_s, ckv_s, kpe_s, wkv_s, sh_s = out_s
    stack = lambda z: z[None]
    return (y_p, y_s, stack(ckv_p), stack(kpe_p), stack(wkv_p), stack(sh_p),
            stack(ckv_s), stack(kpe_s), stack(wkv_s), stack(sh_s))
```

```python
import functools

import jax
import jax.numpy as jnp
from jax import lax
from jax.experimental import pallas as pl
from jax.experimental.pallas import tpu as pltpu

F32 = jnp.float32
BF16 = jnp.bfloat16

D_MODEL = 1024
MLA_HEADS = 8
QK_NOPE = 64
QK_ROPE = 32
V_DIM = 64
Q_LORA = 384
KV_LORA = 256
ROPE_THETA = 10000.0
SM_SCALE = (QK_NOPE + QK_ROPE) ** -0.5
RW_HEADS = 8
RW_HEAD = 64
RW_DIM = RW_HEADS * RW_HEAD
DECAY_LORA = 64
AAA_LORA = 64
GATE_LORA = 128
GN_EPS = RW_HEAD * 1e-5
D_FF = 2816
RMS_EPS = 1e-6
NEG_INF = -1e30
MLA_COLS = Q_LORA + KV_LORA + QK_ROPE
RW_COLS = 3 * RW_DIM + DECAY_LORA + AAA_LORA + GATE_LORA

LANES = 128
ROPE_PAD = LANES
QK_WIDTH = KV_LORA + ROPE_PAD
FF_CHUNKS = 2
VMEM_LIMIT = 56 * 1024 * 1024

_NT = (((1,), (1,)), ((), ()))
_TN = (((0,), (0,)), ((), ()))


def _pick_tile(n, pref):
    t = pref
    while n % t:
        t //= 2
    return t


def _const_spec(shape):
    nd = len(shape)
    return pl.BlockSpec(shape, lambda *_: (0,) * nd, pipeline_mode=pl.Buffered(1))


def _params(sem):
    return pltpu.CompilerParams(dimension_semantics=sem, vmem_limit_bytes=VMEM_LIMIT)


def _dot(a, b):
    return jnp.dot(a, b, preferred_element_type=F32)


def _dot_nt(a, b):
    return lax.dot_general(a, b, _NT, preferred_element_type=F32)


def _dot_tn(a, b):
    return lax.dot_general(a, b, _TN, preferred_element_type=F32)


def _rmsnorm(x, g):
    return x * lax.rsqrt(jnp.mean(x * x, axis=-1, keepdims=True) + RMS_EPS) * g


def _swiglu(hn, wg_ref, wu_ref, wd_ref):
    fc = D_FF // FF_CHUNKS
    acc = None
    for c in range(FF_CHUNKS):
        g = _dot(hn, wg_ref[:, c * fc:(c + 1) * fc])
        u = _dot(hn, wu_ref[:, c * fc:(c + 1) * fc])
        act = (g * jax.nn.sigmoid(g) * u).astype(BF16)
        part = _dot(act, wd_ref[c * fc:(c + 1) * fc, :])
        acc = part if acc is None else acc + part
    return acc


def _ffn1_kernel(x_ref, n1_ref, wg_ref, wu_ref, wd_ref, nm_ref, x1_ref, h_ref):
    x = x_ref[...]
    hn = _rmsnorm(x, n1_ref[...]).astype(BF16)
    x1 = x + 0.5 * _swiglu(hn, wg_ref, wu_ref, wd_ref)
    x1_ref[...] = x1
    h_ref[...] = _rmsnorm(x1, nm_ref[...]).astype(BF16)


def _ffn1(x, n1, wg, wu, wd, nm):
    n = x.shape[0]
    tm = _pick_tile(n, 512)
    row = lambda w: pl.BlockSpec((tm, w), lambda i: (i, 0))
    return pl.pallas_call(
        _ffn1_kernel,
        grid=(n // tm,),
        in_specs=[row(D_MODEL), _const_spec((1, D_MODEL)), _const_spec((D_MODEL, D_FF)),
                  _const_spec((D_MODEL, D_FF)), _const_spec((D_FF, D_MODEL)), _const_spec((1, D_MODEL))],
        out_specs=[row(D_MODEL), row(D_MODEL)],
        out_shape=[jax.ShapeDtypeStruct((n, D_MODEL), F32), jax.ShapeDtypeStruct((n, D_MODEL), BF16)],
        compiler_params=_params(("parallel",)),
        name="ffn1",
    )(x, n1, wg, wu, wd, nm)


def _proj_kernel(h_ref, cos_ref, sin_ref, wmla_ref, wrw_ref, wgate_ref, qn_ref, kvn_ref,
                 wnope_ref, wpe_ref, wper_ref, place_ref, wuk_ref,
                 q_ref, kv_ref, ckv_ref, kpe_ref, urw_ref, gate_ref):
    h = h_ref[...]
    cos = cos_ref[...]
    sin = sin_ref[...]
    u = _dot(h, wmla_ref[...])
    qa = u[:, :Q_LORA]
    c0 = Q_LORA + KV_LORA
    ckv = _rmsnorm(u[:, Q_LORA:c0], kvn_ref[...])
    kpe = u[:, c0:c0 + ROPE_PAD] * cos + u[:, c0 + ROPE_PAD:] * sin
    ckv_ref[...] = ckv
    kpe_ref[...] = kpe[:, :QK_ROPE]
    kv_ref[...] = jnp.concatenate([ckv, kpe], axis=1).astype(BF16)

    qn = _rmsnorm(qa, qn_ref[...]).astype(BF16)
    q_nope = _dot(qn, wnope_ref[...]).astype(BF16)
    pair_w = 2 * QK_NOPE
    q_lat = jnp.concatenate([_dot(q_nope[:, pp * pair_w:(pp + 1) * pair_w], wuk_ref[pp])
                             for pp in range(MLA_HEADS // 2)], axis=1)
    reps = MLA_HEADS * QK_ROPE // ROPE_PAD
    cos_q = jnp.concatenate([cos] * reps, axis=1)
    sin_q = jnp.concatenate([sin] * reps, axis=1)
    q_pe = ((_dot(qn, wpe_ref[...]) * cos_q + _dot(qn, wper_ref[...]) * sin_q) * SM_SCALE).astype(BF16)
    pe_all = _dot(q_pe, place_ref[...])
    for hh in range(MLA_HEADS):
        ql = q_lat[:, hh * KV_LORA:(hh + 1) * KV_LORA] * SM_SCALE
        pe = pe_all[:, hh * ROPE_PAD:(hh + 1) * ROPE_PAD]
        q_ref[hh] = jnp.concatenate([ql, pe], axis=1).astype(BF16)

    urw_ref[...] = _dot(h, wrw_ref[...])
    gate_ref[...] = jax.nn.sigmoid(_dot(h, wgate_ref[...]))


def _proj(h, cos, sin, w):
    n = h.shape[0]
    tm = _pick_tile(cos.shape[0], _pick_tile(n, 512))
    nrb = cos.shape[0] // tm
    row = lambda wd_: pl.BlockSpec((tm, wd_), lambda i: (i, 0))
    rope = pl.BlockSpec((tm, ROPE_PAD), lambda i: (i % nrb, 0))
    wmla_w = Q_LORA + KV_LORA + 2 * ROPE_PAD
    return pl.pallas_call(
        _proj_kernel,
        grid=(n // tm,),
        in_specs=[row(D_MODEL), rope, rope,
                  _const_spec((D_MODEL, wmla_w)), _const_spec((D_MODEL, RW_COLS)),
                  _const_spec((D_MODEL, 2 * D_MODEL)), _const_spec((1, Q_LORA)), _const_spec((1, KV_LORA)),
                  _const_spec((Q_LORA, MLA_HEADS * QK_NOPE)), _const_spec((Q_LORA, MLA_HEADS * QK_ROPE)),
                  _const_spec((Q_LORA, MLA_HEADS * QK_ROPE)),
                  _const_spec((MLA_HEADS * QK_ROPE, MLA_HEADS * ROPE_PAD)),
                  _const_spec((MLA_HEADS // 2, 2 * QK_NOPE, 2 * KV_LORA))],
        out_specs=[pl.BlockSpec((MLA_HEADS, tm, QK_WIDTH), lambda i: (0, i, 0)),
                   row(QK_WIDTH), row(KV_LORA), row(QK_ROPE), row(RW_COLS), row(2 * D_MODEL)],
        out_shape=[jax.ShapeDtypeStruct((MLA_HEADS, n, QK_WIDTH), BF16),
                   jax.ShapeDtypeStruct((n, QK_WIDTH), BF16),
                   jax.ShapeDtypeStruct((n, KV_LORA), F32),
                   jax.ShapeDtypeStruct((n, QK_ROPE), F32),
                   jax.ShapeDtypeStruct((n, RW_COLS), F32),
                   jax.ShapeDtypeStruct((n, 2 * D_MODEL), F32)],
        compiler_params=_params(("parallel",)),
        name="proj",
    )(h, cos, sin, w["w_mla"], w["w_rw"], w["w_gate"], w["q_norm"], w["kv_norm"],
      w["w_nope"], w["w_pe"], w["w_pe_rot"], w["rope_place"], w["w_uk_bd"])


def _lane_tile(x, n):
    if n <= LANES:
        return x[:, :n]
    return jnp.concatenate([x] * (n // LANES), axis=1)


def _softmax_update(s, v, m_ref, l_ref, acc_ref, rows):
    m_prev = m_ref[rows, :]
    m_new = jnp.maximum(m_prev, jnp.max(s, axis=-1, keepdims=True))
    alpha = jnp.exp(m_prev - m_new)
    p = jnp.exp(s - _lane_tile(m_new, s.shape[1]))
    l_ref[rows, :] = alpha * l_ref[rows, :] + jnp.sum(p, axis=-1, keepdims=True)
    pb = p.astype(BF16)
    if isinstance(v, (list, tuple)):
        off, pv = 0, None
        for piece in v:
            part = _dot(pb[:, off:off + piece.shape[0]], piece)
            pv = part if pv is None else pv + part
            off += piece.shape[0]
    else:
        pv = _dot(pb, v)
    acc_ref[rows, :] = _lane_tile(alpha, KV_LORA) * acc_ref[rows, :] + pv
    m_ref[rows, :] = m_new


def _softmax_update_groups(scores, v, m_ref, l_ref, acc_ref, row_groups):
    m_prev = [m_ref[rows, :] for rows in row_groups]
    m_new = [jnp.maximum(mp, jnp.max(s, axis=-1, keepdims=True)) for mp, s in zip(m_prev, scores)]
    alpha = [jnp.exp(mp - mn) for mp, mn in zip(m_prev, m_new)]
    p = [jnp.exp(s - _lane_tile(mn, s.shape[1])) for s, mn in zip(scores, m_new)]
    for rows, al, pg in zip(row_groups, alpha, p):
        l_ref[rows, :] = al * l_ref[rows, :] + jnp.sum(pg, axis=-1, keepdims=True)
    pv = [_dot(pg.astype(BF16), v) for pg in p]
    for rows, al, pvg, mn in zip(row_groups, alpha, pv, m_new):
        acc_ref[rows, :] = _lane_tile(al, KV_LORA) * acc_ref[rows, :] + pvg
        m_ref[rows, :] = mn


def _softmax_init(m_ref, l_ref, acc_ref):
    m_ref[...] = jnp.full(m_ref.shape, NEG_INF, F32)
    l_ref[...] = jnp.zeros(l_ref.shape, F32)
    acc_ref[...] = jnp.zeros(acc_ref.shape, F32)


def _softmax_result(l_ref, acc_ref):
    return acc_ref[...] / _lane_tile(l_ref[...], KV_LORA)


def _mla_out(o_lat, wuv_ref, wo_ref, rows):
    heads = [_dot(o_lat[hh * rows:(hh + 1) * rows].astype(BF16), wuv_ref[hh]).astype(BF16)
             for hh in range(MLA_HEADS)]
    return _dot(jnp.concatenate(heads, axis=1), wo_ref[...])


ATTN_SPLIT = 2


def _attn_prompt_kernel(q_ref, kv_ref, wuv_ref, wo_ref, o_ref, m_ref, l_ref, acc_ref, *, tq, tk):
    i = pl.program_id(1)
    hs = MLA_HEADS // ATTN_SPLIT
    sub = hs * tq
    _softmax_init(m_ref, l_ref, acc_ref)
    n_full = (i * tq) // tk

    def chunks(starts, diagonal_last):
        kvs = [kv_ref[pl.ds(k0, tk), :] for k0 in starts]
        if diagonal_last:
            delta = (lax.broadcasted_iota(jnp.int32, (sub, tk), 1)
                     - lax.broadcasted_iota(jnp.int32, (sub, tk), 0) % tq)
            visible = delta <= i * tq - starts[-1]
        row_groups = [pl.ds(g * sub, sub) for g in range(ATTN_SPLIT)]
        qs = [q_ref[g * hs:(g + 1) * hs].reshape(sub, QK_WIDTH) for g in range(ATTN_SPLIT)]
        scores = [[_dot_nt(q, kv) for q in qs] for kv in kvs]
        if diagonal_last:
            scores[-1] = [jnp.where(visible, s, NEG_INF) for s in scores[-1]]
        for kv, s_groups in zip(kvs, scores):
            _softmax_update_groups(s_groups, kv[:, :KV_LORA], m_ref, l_ref, acc_ref, row_groups)

    def pair_step(j, carry):
        k0 = pl.multiple_of(2 * j * tk, 2 * tk)
        chunks([k0, pl.multiple_of(k0 + tk, tk)], False)
        return carry

    lax.fori_loop(0, n_full // 2, pair_step, 0)
    k_diag = pl.multiple_of(n_full * tk, tk)

    @pl.when(n_full % 2 == 1)
    def _():
        chunks([pl.multiple_of(k_diag - tk, tk), k_diag], True)

    @pl.when(n_full % 2 == 0)
    def _():
        chunks([k_diag], True)

    o_ref[...] = _mla_out(_softmax_result(l_ref, acc_ref), wuv_ref, wo_ref, tq)


def _attn_prompt(q, kv, wuv, wo, bsz, t):
    tq = _pick_tile(t, 256)
    tk = _pick_tile(t, 512)
    nq = t // tq
    rows = MLA_HEADS * tq
    kern = functools.partial(_attn_prompt_kernel, tq=tq, tk=tk)
    return pl.pallas_call(
        kern,
        grid=(bsz, nq),
        in_specs=[pl.BlockSpec((MLA_HEADS, tq, QK_WIDTH), lambda b, i: (0, b * nq + i, 0)),
                  pl.BlockSpec((t, QK_WIDTH), lambda b, i: (b, 0)),
                  _const_spec((MLA_HEADS, KV_LORA, V_DIM)), _const_spec((MLA_HEADS * V_DIM, D_MODEL))],
        out_specs=pl.BlockSpec((tq, D_MODEL), lambda b, i: (b * nq + i, 0)),
        out_shape=jax.ShapeDtypeStruct((bsz * t, D_MODEL), F32),
        scratch_shapes=[pltpu.VMEM((rows, LANES), F32), pltpu.VMEM((rows, LANES), F32),
                        pltpu.VMEM((rows, KV_LORA), F32)],
        compiler_params=_params(("parallel", "arbitrary")),
        name="attn_prompt",
    )(q, kv, wuv, wo)


SAMPLE_SLOTS = 4
SAMPLE_PARTS = 2

def _attn_sample_kernel(pt_ref, q_ref, kvn_ref, cc_hbm, ck_hbm, o_ref,
                        cbuf, kbuf, csem, ksem, m_ref, l_ref, acc_ref, *, pgc, n_pages, tnew):
    b = pl.program_id(0)
    nch = n_pages // pgc
    ahead = SAMPLE_SLOTS - 2
    page_rows = cbuf.shape[2]

    def chunk_copies(seq, ch):
        slot = ch % SAMPLE_SLOTS
        out = []
        for p in range(pgc):
            page = pt_ref[seq * n_pages + ch * pgc + p]
            out.append(pltpu.make_async_copy(cc_hbm.at[page], cbuf.at[slot, p], csem.at[slot]))
            out.append(pltpu.make_async_copy(ck_hbm.at[page], kbuf.at[slot, p], ksem.at[slot]))
        return out

    def start_chunk(ch):
        if ch < nch:
            for cp in chunk_copies(b, ch):
                cp.start()
        else:
            @pl.when(b + 1 < pl.num_programs(0))
            def _():
                for cp in chunk_copies(b + 1, ch - nch):
                    cp.start()

    @pl.when(b == 0)
    def _():
        for ch in range(ahead):
            for cp in chunk_copies(b, ch):
                cp.start()

    _softmax_init(m_ref, l_ref, acc_ref)
    q = q_ref[...]
    rows = q.shape[0]
    all_rows = pl.ds(0, rows)
    q_lat = q[:, :KV_LORA]
    q_pe = q[:, KV_LORA:KV_LORA + QK_ROPE]

    scored = None
    for ch in range(nch):
        start_chunk(ch + ahead)
        for cp in chunk_copies(b, ch):
            cp.wait()
        slot = ch % SAMPLE_SLOTS
        part = pgc // SAMPLE_PARTS
        cs = [cbuf[slot, h * part:(h + 1) * part].reshape(part * page_rows, KV_LORA).astype(BF16)
              for h in range(SAMPLE_PARTS)]
        kt = jnp.concatenate([kbuf[slot, p] for p in range(pgc)], axis=1).astype(BF16)
        s = jnp.concatenate([_dot_nt(q_lat, c) for c in cs], axis=1) + _dot(q_pe, kt)
        if scored is not None:
            _softmax_update(scored[0], scored[1], m_ref, l_ref, acc_ref, all_rows)
        scored = (s, cs)
    _softmax_update(scored[0], scored[1], m_ref, l_ref, acc_ref, all_rows)

    kvn = kvn_ref[...].astype(BF16)
    sn = _dot_nt(q, kvn)
    qt = lax.broadcasted_iota(jnp.int32, (rows, tnew), 0) % tnew
    kt_ = lax.broadcasted_iota(jnp.int32, (rows, tnew), 1)
    sn = jnp.where(kt_ <= qt, sn, NEG_INF)
    _softmax_update(sn, kvn[:, :KV_LORA], m_ref, l_ref, acc_ref, all_rows)
    o_ref[...] = _softmax_result(l_ref, acc_ref)


def _attn_sample(page_table, q, kvn, cache_c, cache_kt):
    nb, n_pages = page_table.shape
    tnew = kvn.shape[1]
    rows = q.shape[1]
    page = cache_c.shape[1]
    pgc = _pick_tile(n_pages, 32)
    assert (n_pages // pgc) % SAMPLE_SLOTS == 0, "ring slots are static per chunk, also across sequences"
    kern = functools.partial(_attn_sample_kernel, pgc=pgc, n_pages=n_pages, tnew=tnew)
    grid_spec = pltpu.PrefetchScalarGridSpec(
        num_scalar_prefetch=1,
        grid=(nb,),
        in_specs=[pl.BlockSpec((None, rows, QK_WIDTH), lambda b, pt: (b, 0, 0)),
                  pl.BlockSpec((None, tnew, QK_WIDTH), lambda b, pt: (b, 0, 0)),
                  pl.BlockSpec(memory_space=pl.ANY), pl.BlockSpec(memory_space=pl.ANY)],
        out_specs=pl.BlockSpec((None, rows, KV_LORA), lambda b, pt: (b, 0, 0)),
        scratch_shapes=[pltpu.VMEM((SAMPLE_SLOTS, pgc, page, KV_LORA), F32),
                        pltpu.VMEM((SAMPLE_SLOTS, pgc, QK_ROPE, page), F32),
                        pltpu.SemaphoreType.DMA((SAMPLE_SLOTS,)), pltpu.SemaphoreType.DMA((SAMPLE_SLOTS,)),
                        pltpu.VMEM((rows, LANES), F32), pltpu.VMEM((rows, LANES), F32),
                        pltpu.VMEM((rows, KV_LORA), F32)],
    )
    return pl.pallas_call(
        kern,
        grid_spec=grid_spec,
        out_shape=jax.ShapeDtypeStruct((nb, rows, KV_LORA), F32),
        compiler_params=_params(("arbitrary",)),
        name="attn_sample",
    )(page_table.reshape(-1), q, kvn, cache_c, cache_kt)


def _mla_out_kernel(ol_ref, wuv_ref, wo_ref, o_ref):
    tm = ol_ref.shape[1]
    o_ref[...] = _mla_out(ol_ref[...].reshape(MLA_HEADS * tm, KV_LORA), wuv_ref, wo_ref, tm)


def _mla_out_call(o_lat, wuv, wo):
    n = o_lat.shape[1]
    tm = _pick_tile(n, 256)
    return pl.pallas_call(
        _mla_out_kernel,
        grid=(n // tm,),
        in_specs=[pl.BlockSpec((MLA_HEADS, tm, KV_LORA), lambda i: (0, i, 0)),
                  _const_spec((MLA_HEADS, KV_LORA, V_DIM)), _const_spec((MLA_HEADS * V_DIM, D_MODEL))],
        out_specs=pl.BlockSpec((tm, D_MODEL), lambda i: (i, 0)),
        out_shape=jax.ShapeDtypeStruct((n, D_MODEL), F32),
        compiler_params=_params(("parallel",)),
        name="mla_out",
    )(o_lat, wuv, wo)


def _split_dot(x, w_bf, pieces):
    out = None
    rem = x
    for _ in range(pieces):
        hi = rem.astype(BF16)
        part = _dot(hi, w_bf)
        out = part if out is None else out + part
        rem = rem - hi.astype(F32)
    return out


def _split_dot_lhs(tri_bf, x):
    out = None
    rem = x
    for _ in range(3):
        hi = rem.astype(BF16)
        part = _dot(tri_bf, hi)
        out = part if out is None else out + part
        rem = rem - hi.astype(F32)
    return out


def _expand_heads(x, hg):
    lane_head = lax.broadcasted_iota(jnp.int32, x.shape, 1) // RW_HEAD
    zero = jnp.zeros((), x.dtype)
    return jnp.concatenate([jnp.where(lane_head == hh, x, zero) for hh in range(hg)], axis=0)


def _fold_heads(x, hg):
    c = x.shape[0] // hg
    out = x[0:c]
    for hh in range(1, hg):
        out = out + x[hh * c:(hh + 1) * c]
    return out


def _rwkv_kernel(u_ref, sh_ref, s0_ref, mu_ref, w0_ref, w2a2_ref, a0_ref, g2_ref, kkw_ref, ka_ref, rk_ref,
                 lnw_ref, lnb_ref, ones_ref, wo_ref,
                 ob_ref, sout_ref,
                 state_ref, prev_ref, p_ref, mk_ref, *, c, hg, nseq):
    t = pl.program_id(1)
    tt = u_ref.shape[1]
    nc = tt // c
    ng = RW_HEADS // hg
    w = hg * RW_HEAD
    r_ = hg * c

    blockmask = (lax.broadcasted_iota(jnp.int32, (w, w), 0) // RW_HEAD
                 == lax.broadcasted_iota(jnp.int32, (w, w), 1) // RW_HEAD)

    @pl.when(t == 0)
    def _():
        prev_ref[...] = sh_ref[...]
        for s in range(nseq):
            for g in range(ng):
                s0g = s0_ref[s, :, g * w:(g + 1) * w]
                state_ref[s, g] = jnp.where(blockmask, jnp.concatenate([s0g] * hg, axis=0), 0.0)

    u = u_ref[...].reshape(nseq * tt, RW_COLS)
    row = lax.broadcasted_iota(jnp.int32, u.shape, 0)
    u_prev = pltpu.roll(u, 1, 0)
    for s in range(nseq):
        u_prev = jnp.where(row == s * tt, prev_ref[s], u_prev)
        prev_ref[s] = u[(s + 1) * tt - 1:(s + 1) * tt, :]
    us = u + (u_prev - u) * mu_ref[...]
    r = us[:, 0:RW_DIM]
    k = us[:, RW_DIM:2 * RW_DIM]
    v = us[:, 2 * RW_DIM:3 * RW_DIM]
    o1 = 3 * RW_DIM
    wa = us[:, o1:o1 + DECAY_LORA + AAA_LORA]
    gd = us[:, o1 + DECAY_LORA + AAA_LORA:]
    lane = lax.broadcasted_iota(jnp.int32, wa.shape, 1)
    wa = jnp.where(lane < DECAY_LORA, jnp.tanh(wa), wa)
    lo = _dot(wa.astype(BF16), w2a2_ref[...])
    w_raw = -jax.nn.softplus(-(w0_ref[...] + lo[:, :RW_DIM])) - 0.5
    logw = -jnp.exp(w_raw)
    a = jax.nn.sigmoid(a0_ref[...] + lo[:, RW_DIM:])
    g = _dot(jax.nn.sigmoid(gd).astype(BF16), g2_ref[...])
    ones_bd = ones_ref[...]
    kk = k * kkw_ref[...]
    kk = kk / jnp.maximum(jnp.sqrt(_split_dot(kk * kk, ones_bd, 1)), 1e-12)
    k2 = k * (1.0 + (a - 1.0) * ka_ref[...])

    tri = (lax.broadcasted_iota(jnp.int32, (c, c), 1) <= lax.broadcasted_iota(jnp.int32, (c, c), 0)).astype(BF16)
    rr = lax.broadcasted_iota(jnp.int32, (r_, r_), 0)
    cc = lax.broadcasted_iota(jnp.int32, (r_, r_), 1)
    strict_bd = (rr // c == cc // c) & (cc % c < rr % c)
    eye = (rr == cc).astype(F32)
    incl_wide = (lax.broadcasted_iota(jnp.int32, (c, r_), 1) % c
                 <= lax.broadcasted_iota(jnp.int32, (c, r_), 0))

    inst = [(s, ci, g_) for s in range(nseq) for ci in range(nc) for g_ in range(ng)]
    number = {key: idx for idx, key in enumerate(inst)}
    pre = {}
    for s in range(nseq):
        for ci in range(nc):
            rows = slice(s * tt + ci * c, s * tt + (ci + 1) * c)
            lw = logw[rows]
            cs = _split_dot_lhs(tri, lw)
            e_cur = jnp.exp(cs)
            e_inv = jnp.exp(-cs)
            kkc = kk[rows]
            at = (-kkc * jnp.exp(cs - lw)).astype(BF16)
            bt = (kkc * a[rows] * e_inv).astype(BF16)
            kt = (k2[rows] * e_inv).astype(BF16)
            rt = (r[rows] * e_cur).astype(BF16)
            vb = v[rows].astype(BF16)
            for g_ in range(ng):
                sl = slice(g_ * w, (g_ + 1) * w)
                pre[s, ci, g_] = dict(a=at[:, sl], b=bt[:, sl], k=kt[:, sl], r=rt[:, sl], v=vb[:, sl],
                                      p_end=e_cur[c - 1:c, sl])
    for key in inst:
        z = pre[key]
        ax, bx, kx = (_expand_heads(z[name], hg) for name in ("a", "b", "k"))
        z["n1"] = _dot_nt(jnp.concatenate([ax, z["r"]], axis=0), jnp.concatenate([bx, kx], axis=0))
    for key in inst:
        z = pre[key]
        n1 = z.pop("n1")
        m_ab = jnp.where(strict_bd, n1[:r_, :r_], 0.0)
        p_ref[number[key]] = eye + m_ab
        mk_ref[number[key]] = m_ab.astype(BF16)
        z["m_ak"] = jnp.where(strict_bd, n1[:r_, r_:], 0.0).astype(BF16)
        z["a_rb"] = jnp.where(incl_wide, n1[r_:, :r_], 0.0).astype(BF16)
        z["a_rk"] = jnp.where(incl_wide, n1[r_:, r_:], 0.0).astype(BF16)
    for key in inst:
        z = pre[key]
        vx = _expand_heads(z["v"], hg)
        z["wv"] = _dot(z.pop("m_ak"), vx)
        z["yv"] = _dot(z.pop("a_rk"), vx)

    def inverse_level(_, carry):
        squares = []
        for idx in range(len(inst)):
            mk = mk_ref[idx]
            squares.append(_dot(mk, mk).astype(BF16))
        for idx, mk2 in enumerate(squares):
            p = p_ref[idx]
            p_ref[idx] = p + _dot(p.astype(BF16), mk2)
            mk_ref[idx] = mk2
        return carry

    levels = max(c.bit_length() - 2, 0)
    lax.fori_loop(0, levels, inverse_level, 0)

    y_part = {}
    for ci in range(nc):
        keys = [(s, ci, g_) for s in range(nseq) for g_ in range(ng)]
        state = {key: state_ref[key[0], key[2]] for key in keys}
        sa = {key: _dot_nt(jnp.concatenate([pre[key]["a"], pre[key]["r"]], axis=0), state[key].astype(BF16))
              for key in keys}
        ux = {key: _dot(p_ref[number[key]].astype(BF16),
                        (_expand_heads(sa[key][:c], hg) + pre[key]["wv"]).astype(BF16))
              for key in keys}
        for key in keys:
            z = pre[key]
            y_part[key] = sa[key][c:] + _dot(z["a_rb"], ux[key].astype(BF16)) + z["yv"]
        ds = {key: _dot_tn(jnp.concatenate([_fold_heads(ux[key], hg).astype(BF16), pre[key]["v"]], axis=0),
                           jnp.concatenate([pre[key]["b"], pre[key]["k"]], axis=0))
              for key in keys}
        for key in keys:
            state_ref[key[0], key[2]] = (state[key] + jnp.where(blockmask, ds[key], 0.0)) * pre[key]["p_end"]
    y_rows = []
    for s in range(nseq):
        for ci in range(nc):
            ys = [y_part[s, ci, g_] for g_ in range(ng)]
            y_rows.append(ys[0] if ng == 1 else jnp.concatenate(ys, axis=1))
    y = y_rows[0] if len(y_rows) == 1 else jnp.concatenate(y_rows, axis=0)

    inv_n = 1.0 / RW_HEAD
    mean = _split_dot(y, ones_bd, 2) * inv_n
    d = y - mean
    var = _split_dot(d * d, ones_bd, 1) * inv_n
    yn = d * lax.rsqrt(var + GN_EPS) * lnw_ref[...] + lnb_ref[...]
    bonus = _split_dot(r * k2 * rk_ref[...], ones_bd, 1) * v
    ob_ref[...] = _dot(((yn + bonus) * g).astype(BF16), wo_ref[...]).reshape(nseq, tt, D_MODEL)

    @pl.when(t == pl.num_programs(1) - 1)
    def _():
        for s in range(nseq):
            for g_ in range(ng):
                sout_ref[s, :, g_ * w:(g_ + 1) * w] = _fold_heads(state_ref[s, g_], hg)


RW_STEP_TOKENS = 512


def _rwkv(u_rw, shift_in, s0, w, bsz, t):
    if t >= 64:
        c, hg, nseq = 64, 4, _pick_tile(bsz, 8)
        tt = _pick_tile(t, max(c, RW_STEP_TOKENS // nseq))
    else:
        c, hg, nseq = t, RW_HEADS, _pick_tile(bsz, 8)
        tt = t
    nt = t // tt
    ng = RW_HEADS // hg
    wd_ = hg * RW_HEAD
    n_inst = nseq * (tt // c) * ng
    kern = functools.partial(_rwkv_kernel, c=c, hg=hg, nseq=nseq)
    vec = lambda n: _const_spec((1, n))
    o_b, s_new = pl.pallas_call(
        kern,
        grid=(bsz // nseq, nt),
        in_specs=[pl.BlockSpec((nseq, tt, RW_COLS), lambda b, i: (b, i, 0)),
                  pl.BlockSpec((nseq, 1, RW_COLS), lambda b, i: (b, 0, 0)),
                  pl.BlockSpec((nseq, RW_HEAD, RW_DIM), lambda b, i: (b, 0, 0)),
                  vec(RW_COLS), vec(RW_DIM), _const_spec((DECAY_LORA + AAA_LORA, 2 * RW_DIM)), vec(RW_DIM),
                  _const_spec((GATE_LORA, RW_DIM)), vec(RW_DIM), vec(RW_DIM), vec(RW_DIM), vec(RW_DIM),
                  vec(RW_DIM), _const_spec((RW_DIM, RW_DIM)), _const_spec((RW_DIM, D_MODEL))],
        out_specs=[pl.BlockSpec((nseq, tt, D_MODEL), lambda b, i: (b, i, 0)),
                   pl.BlockSpec((nseq, RW_HEAD, RW_DIM), lambda b, i: (b, 0, 0))],
        out_shape=[jax.ShapeDtypeStruct((bsz, t, D_MODEL), F32),
                   jax.ShapeDtypeStruct((bsz, RW_HEAD, RW_DIM), F32)],
        scratch_shapes=[pltpu.VMEM((nseq, ng, wd_, wd_), F32), pltpu.VMEM((nseq, 1, RW_COLS), F32),
                        pltpu.VMEM((n_inst, hg * c, hg * c), F32), pltpu.VMEM((n_inst, hg * c, hg * c), BF16)],
        compiler_params=_params(("parallel", "arbitrary")),
        name="rwkv",
    )(u_rw.reshape(bsz, t, RW_COLS), shift_in, s0, w["rw_mu"], w["rw_w0"], w["rw_w2a2"], w["rw_a0"], w["rw_g2"],
      w["rw_kk"], w["rw_ka"], w["rw_rk"], w["rw_ln_w"], w["rw_ln_b"], w["ones_bd"], w["w_o_rw"])
    return o_b.reshape(bsz * t, D_MODEL), s_new


def _merge_kernel(x1_ref, oa_ref, ob_ref, gate_ref, wout_ref, n2_ref, wg_ref, wu_ref, wd_ref, nf_ref, y_ref):
    mixed = (gate_ref[:, :D_MODEL] * oa_ref[...] + gate_ref[:, D_MODEL:] * ob_ref[...]).astype(BF16)
    x2 = x1_ref[...] + _dot(mixed, wout_ref[...])
    hn = _rmsnorm(x2, n2_ref[...]).astype(BF16)
    x3 = x2 + 0.5 * _swiglu(hn, wg_ref, wu_ref, wd_ref)
    y_ref[...] = _rmsnorm(x3, nf_ref[...])


def _merge(x1, oa, ob, gate, wout, n2, wg, wu, wd, nf):
    n = x1.shape[0]
    tm = _pick_tile(n, 256)
    row = lambda w_: pl.BlockSpec((tm, w_), lambda i: (i, 0))
    return pl.pallas_call(
        _merge_kernel,
        grid=(n // tm,),
        in_specs=[row(D_MODEL), row(D_MODEL), row(D_MODEL), row(2 * D_MODEL),
                  _const_spec((D_MODEL, D_MODEL)), _const_spec((1, D_MODEL)), _const_spec((D_MODEL, D_FF)),
                  _const_spec((D_MODEL, D_FF)), _const_spec((D_FF, D_MODEL)), _const_spec((1, D_MODEL))],
        out_specs=row(D_MODEL),
        out_shape=jax.ShapeDtypeStruct((n, D_MODEL), F32),
        compiler_params=_params(("parallel",)),
        name="merge_ffn2",
    )(x1, oa, ob, gate, wout, n2, wg, wu, wd, nf)


def _rope_tables(pos):
    inv = 1.0 / (ROPE_THETA ** (jnp.arange(0, QK_ROPE, 2, dtype=F32) / QK_ROPE))
    ang = pos.astype(F32)[:, None] * inv[None, :]
    cos, sin = jnp.cos(ang), jnp.sin(ang)
    reps = 2 * ROPE_PAD // QK_ROPE
    return jnp.concatenate([cos] * reps, axis=1), jnp.concatenate([sin] * reps, axis=1)


def _rot_cols(wm):
    half = QK_ROPE // 2
    return jnp.concatenate([-wm[..., half:], wm[..., :half]], axis=-1)


def _pad_cols(wm, width):
    return jnp.pad(wm, [(0, 0)] * (wm.ndim - 1) + [(0, width - wm.shape[-1])])


def _prep_weights(p):
    w_in = p["w_in"]
    w_kpe = w_in[:, Q_LORA + KV_LORA:MLA_COLS]
    w_mla = jnp.concatenate([w_in[:, :Q_LORA + KV_LORA], _pad_cols(w_kpe, ROPE_PAD),
                             _pad_cols(_rot_cols(w_kpe), ROPE_PAD)], axis=1)
    w_qb = p["w_qb"].reshape(Q_LORA, MLA_HEADS, QK_NOPE + QK_ROPE)
    w_nope = w_qb[:, :, :QK_NOPE].reshape(Q_LORA, MLA_HEADS * QK_NOPE)
    w_pe = w_qb[:, :, QK_NOPE:]
    w_uk = jnp.transpose(p["w_uk"], (1, 2, 0)).reshape(MLA_HEADS // 2, 2, QK_NOPE, KV_LORA)
    same_head = jnp.eye(2, dtype=bool)[None, :, None, :, None]
    w_uk_bd = jnp.where(same_head, w_uk[:, :, :, None, :], 0.0).reshape(MLA_HEADS // 2, 2 * QK_NOPE, 2 * KV_LORA)
    zeros = jnp.zeros((DECAY_LORA, RW_DIM), F32)
    w2a2 = jnp.concatenate([jnp.concatenate([p["rw_w2"], zeros], axis=1),
                            jnp.concatenate([zeros, p["rw_a2"]], axis=1)], axis=0)
    head = jnp.arange(RW_DIM) // RW_HEAD
    rope_lane = jnp.arange(MLA_HEADS * QK_ROPE)
    rope_src = (rope_lane // QK_ROPE) * ROPE_PAD + rope_lane % QK_ROPE
    row = lambda v: v.reshape(1, -1)
    return {
        "w_mla": w_mla.astype(BF16),
        "w_rw": w_in[:, MLA_COLS:MLA_COLS + RW_COLS].astype(BF16),
        "w_gate": w_in[:, MLA_COLS + RW_COLS:].astype(BF16),
        "q_norm": row(p["q_norm"]), "kv_norm": row(p["kv_norm"]),
        "w_nope": w_nope.astype(BF16),
        "w_pe": w_pe.reshape(Q_LORA, MLA_HEADS * QK_ROPE).astype(BF16),
        "w_pe_rot": _rot_cols(w_pe).reshape(Q_LORA, MLA_HEADS * QK_ROPE).astype(BF16),
        "rope_place": (rope_src[:, None] == jnp.arange(MLA_HEADS * ROPE_PAD)[None, :]).astype(BF16),
        "w_uk_bd": w_uk_bd.astype(BF16),
        "w_uv": jnp.transpose(p["w_uv"], (1, 0, 2)).astype(BF16),
        "w_o_mla": p["w_o_mla"].astype(BF16),
        "rw_mu": row(p["rw_mu"]), "rw_w0": row(p["rw_w0"]), "rw_w2a2": w2a2.astype(BF16),
        "rw_a0": row(p["rw_a0"]), "rw_g2": p["rw_g2"].astype(BF16), "rw_kk": row(p["rw_kk"]),
        "rw_ka": row(p["rw_ka"]), "rw_rk": row(p["rw_rk"]), "rw_ln_w": row(p["rw_ln_w"]),
        "rw_ln_b": row(p["rw_ln_b"]),
        "ones_bd": (head[:, None] == head[None, :]).astype(BF16),
        "w_o_rw": p["w_o_rw"].astype(BF16),
        "w_out": p["w_out"].astype(BF16),
        "ffn1_norm": row(p["ffn1_norm"]), "mix_norm": row(p["mix_norm"]), "ffn2_norm": row(p["ffn2_norm"]),
        "ffn1_wg": p["ffn1_wg"].astype(BF16), "ffn1_wu": p["ffn1_wu"].astype(BF16),
        "ffn1_wd": p["ffn1_wd"].astype(BF16),
        "ffn2_wg": p["ffn2_wg"].astype(BF16), "ffn2_wu": p["ffn2_wu"].astype(BF16),
        "ffn2_wd": p["ffn2_wd"].astype(BF16),
    }


def _state_to_lanes(s):
    b = s.shape[0]
    return jnp.transpose(s, (0, 2, 1, 3)).reshape(b, RW_HEAD, RW_DIM)


def _state_from_lanes(s):
    b = s.shape[0]
    return jnp.transpose(s.reshape(b, RW_HEAD, RW_HEADS, RW_HEAD), (0, 2, 1, 3))


def _layer(x, cos, sin, shift_in, s0, w, final_norm, attend):
    bsz, t, _ = x.shape
    n = bsz * t
    x1, h = _ffn1(x.reshape(n, D_MODEL), w["ffn1_norm"], w["ffn1_wg"], w["ffn1_wu"], w["ffn1_wd"], w["mix_norm"])
    q, kv, ckv, kpe, u_rw, gate = _proj(h, cos, sin, w)
    o_a = attend(q, kv)
    o_b, s_new = _rwkv(u_rw, shift_in.reshape(bsz, 1, RW_COLS), _state_to_lanes(s0), w, bsz, t)
    y = _merge(x1, o_a, o_b, gate, w["w_out"], w["ffn2_norm"], w["ffn2_wg"], w["ffn2_wu"], w["ffn2_wd"],
               final_norm.reshape(1, D_MODEL))
    return (y.reshape(bsz, t, D_MODEL), ckv.reshape(bsz, t, KV_LORA), kpe.reshape(bsz, t, QK_ROPE),
            _state_from_lanes(s_new), u_rw.reshape(bsz, t, RW_COLS)[:, -1])


def kernel(x_prompt, x_sample, cache_ckv, cache_kpe, state_wkv, state_shift, page_table, ffn1_norm, ffn1_wg,
           ffn1_wu, ffn1_wd, mix_norm, w_in, q_norm, kv_norm, w_qb, w_uk, w_uv, w_o_mla, rw_mu, rw_w0, rw_w2,
           rw_a0, rw_a2, rw_g2, rw_kk, rw_ka, rw_rk, rw_ln_w, rw_ln_b, w_o_rw, w_out, ffn2_norm, ffn2_wg,
           ffn2_wu, ffn2_wd, final_norm):
    depth = w_in.shape[0]
    assert depth == 1, "the final norm is fused into the layer's last stage"
    layer_params = dict(ffn1_norm=ffn1_norm, ffn1_wg=ffn1_wg, ffn1_wu=ffn1_wu, ffn1_wd=ffn1_wd, mix_norm=mix_norm,
                        w_in=w_in, q_norm=q_norm, kv_norm=kv_norm, w_qb=w_qb, w_uk=w_uk, w_uv=w_uv,
                        w_o_mla=w_o_mla, rw_mu=rw_mu, rw_w0=rw_w0, rw_w2=rw_w2, rw_a0=rw_a0, rw_a2=rw_a2,
                        rw_g2=rw_g2, rw_kk=rw_kk, rw_ka=rw_ka, rw_rk=rw_rk, rw_ln_w=rw_ln_w, rw_ln_b=rw_ln_b,
                        w_o_rw=w_o_rw, w_out=w_out, ffn2_norm=ffn2_norm, ffn2_wg=ffn2_wg, ffn2_wu=ffn2_wu,
                        ffn2_wd=ffn2_wd)
    w = _prep_weights({k: v[0] for k, v in layer_params.items()})

    bp, tp, _ = x_prompt.shape
    bs, ts, _ = x_sample.shape
    n_pages = page_table.shape[1]
    past_len = n_pages * cache_ckv.shape[2]

    cos_p, sin_p = _rope_tables(jnp.arange(tp, dtype=jnp.int32))
    attend_p = lambda q, kv: _attn_prompt(q, kv, w["w_uv"], w["w_o_mla"], bp, tp)
    out_p = _layer(x_prompt, cos_p, sin_p, jnp.zeros((bp, RW_COLS), F32),
                   jnp.zeros((bp, RW_HEADS, RW_HEAD, RW_HEAD), F32), w, final_norm, attend_p)

    n_s = bs * ts
    tile_s = _pick_tile(n_s, 512)
    pos_s = past_len + jnp.arange(tile_s, dtype=jnp.int32) % ts
    cos_s, sin_s = _rope_tables(pos_s)

    def attend_s(q, kv):
        q_b = jnp.transpose(q.reshape(MLA_HEADS, bs, ts, QK_WIDTH), (1, 0, 2, 3)).reshape(bs, MLA_HEADS * ts, QK_WIDTH)
        kv_new = kv.astype(F32).reshape(bs, ts, QK_WIDTH)
        o_lat = _attn_sample(page_table, q_b, kv_new, cache_ckv[0], jnp.swapaxes(cache_kpe[0], 1, 2))
        o_lat = jnp.transpose(o_lat.reshape(bs, MLA_HEADS, ts, KV_LORA), (1, 0, 2, 3)).reshape(MLA_HEADS, n_s, KV_LORA)
        return _mla_out_call(o_lat, w["w_uv"], w["w_o_mla"])

    out_s = _layer(x_sample, cos_s, sin_s, state_shift[0], state_wkv[0], w, final_norm, attend_s)

    y_p, ckv_p, kpe_p, wkv_p, sh_p = out_p
    y_s, ckv_s, kpe_s, wkv_s, sh_s = out_s
    stack = lambda z: z[None]
    return (y_p, y_s, stack(ckv_p), stack(kpe_p), stack(wkv_p), stack(sh_p),
            stack(ckv_s), stack(kpe_s), stack(wkv_s), stack(sh_s))
```

```python
import functools

import jax
import jax.numpy as jnp
from jax import lax
from jax.experimental import pallas as pl
from jax.experimental.pallas import tpu as pltpu

F32 = jnp.float32
BF16 = jnp.bfloat16

D_MODEL = 1024
MLA_HEADS = 8
QK_NOPE = 64
QK_ROPE = 32
V_DIM = 64
Q_LORA = 384
KV_LORA = 256
ROPE_THETA = 10000.0
SM_SCALE = (QK_NOPE + QK_ROPE) ** -0.5
RW_HEADS = 8
RW_HEAD = 64
RW_DIM = RW_HEADS * RW_HEAD
DECAY_LORA = 64
AAA_LORA = 64
GATE_LORA = 128
GN_EPS = RW_HEAD * 1e-5
D_FF = 2816
RMS_EPS = 1e-6
NEG_INF = -1e30
MLA_COLS = Q_LORA + KV_LORA + QK_ROPE
RW_COLS = 3 * RW_DIM + DECAY_LORA + AAA_LORA + GATE_LORA

LANES = 128
ROPE_PAD = LANES
QK_WIDTH = KV_LORA + ROPE_PAD
FF_CHUNKS = 1
VMEM_LIMIT = 56 * 1024 * 1024

_NT = (((1,), (1,)), ((), ()))
_TN = (((0,), (0,)), ((), ()))


def _pick_tile(n, pref):
    t = pref
    while n % t:
        t //= 2
    return t


def _const_spec(shape):
    nd = len(shape)
    return pl.BlockSpec(shape, lambda *_: (0,) * nd, pipeline_mode=pl.Buffered(1))


def _params(sem):
    return pltpu.CompilerParams(dimension_semantics=sem, vmem_limit_bytes=VMEM_LIMIT)


def _dot(a, b):
    return jnp.dot(a, b, preferred_element_type=F32)


def _dot_nt(a, b):
    return lax.dot_general(a, b, _NT, preferred_element_type=F32)


def _dot_tn(a, b):
    return lax.dot_general(a, b, _TN, preferred_element_type=F32)


def _rmsnorm(x, g):
    return x * lax.rsqrt(jnp.mean(x * x, axis=-1, keepdims=True) + RMS_EPS) * g


def _swiglu(hn, wg_ref, wu_ref, wd_ref):
    fc = D_FF // FF_CHUNKS
    acc = None
    for c in range(FF_CHUNKS):
        g = _dot(hn, wg_ref[:, c * fc:(c + 1) * fc])
        u = _dot(hn, wu_ref[:, c * fc:(c + 1) * fc])
        act = (g * jax.nn.sigmoid(g) * u).astype(BF16)
        part = _dot(act, wd_ref[c * fc:(c + 1) * fc, :])
        acc = part if acc is None else acc + part
    return acc


def _ffn1_kernel(x_ref, n1_ref, wg_ref, wu_ref, wd_ref, nm_ref, x1_ref, h_ref):
    x = x_ref[...]
    hn = _rmsnorm(x, n1_ref[...]).astype(BF16)
    x1 = x + 0.5 * _swiglu(hn, wg_ref, wu_ref, wd_ref)
    x1_ref[...] = x1
    h_ref[...] = _rmsnorm(x1, nm_ref[...]).astype(BF16)


def _ffn1(x, n1, wg, wu, wd, nm):
    n = x.shape[0]
    tm = _pick_tile(n, 512)
    row = lambda w: pl.BlockSpec((tm, w), lambda i: (i, 0))
    return pl.pallas_call(
        _ffn1_kernel,
        grid=(n // tm,),
        in_specs=[row(D_MODEL), _const_spec((1, D_MODEL)), _const_spec((D_MODEL, D_FF)),
                  _const_spec((D_MODEL, D_FF)), _const_spec((D_FF, D_MODEL)), _const_spec((1, D_MODEL))],
        out_specs=[row(D_MODEL), row(D_MODEL)],
        out_shape=[jax.ShapeDtypeStruct((n, D_MODEL), F32), jax.ShapeDtypeStruct((n, D_MODEL), BF16)],
        compiler_params=_params(("parallel",)),
        name="ffn1",
    )(x, n1, wg, wu, wd, nm)


def _proj_kernel(h_ref, cos_ref, sin_ref, wmla_ref, wrw_ref, wgate_ref, qn_ref, kvn_ref,
                 wnope_ref, wpe_ref, wper_ref, place_ref, wuk_ref,
                 q_ref, kv_ref, ckv_ref, kpe_ref, urw_ref, gate_ref):
    h = h_ref[...]
    cos = cos_ref[...]
    sin = sin_ref[...]
    u = _dot(h, wmla_ref[...])
    qa = u[:, :Q_LORA]
    c0 = Q_LORA + KV_LORA
    ckv = _rmsnorm(u[:, Q_LORA:c0], kvn_ref[...])
    kpe = u[:, c0:c0 + ROPE_PAD] * cos + u[:, c0 + ROPE_PAD:] * sin
    ckv_ref[...] = ckv
    kpe_ref[...] = kpe[:, :QK_ROPE]
    kv_ref[...] = jnp.concatenate([ckv, kpe], axis=1).astype(BF16)

    qn = _rmsnorm(qa, qn_ref[...]).astype(BF16)
    q_nope = _dot(qn, wnope_ref[...]).astype(BF16)
    pair_w = 2 * QK_NOPE
    q_lat = jnp.concatenate([_dot(q_nope[:, pp * pair_w:(pp + 1) * pair_w], wuk_ref[pp])
                             for pp in range(MLA_HEADS // 2)], axis=1)
    reps = MLA_HEADS * QK_ROPE // ROPE_PAD
    cos_q = jnp.concatenate([cos] * reps, axis=1)
    sin_q = jnp.concatenate([sin] * reps, axis=1)
    q_pe = ((_dot(qn, wpe_ref[...]) * cos_q + _dot(qn, wper_ref[...]) * sin_q) * SM_SCALE).astype(BF16)
    pe_all = _dot(q_pe, place_ref[...])
    for hh in range(MLA_HEADS):
        ql = q_lat[:, hh * KV_LORA:(hh + 1) * KV_LORA] * SM_SCALE
        pe = pe_all[:, hh * ROPE_PAD:(hh + 1) * ROPE_PAD]
        q_ref[hh] = jnp.concatenate([ql, pe], axis=1).astype(BF16)

    urw_ref[...] = _dot(h, wrw_ref[...])
    gate_ref[...] = jax.nn.sigmoid(_dot(h, wgate_ref[...]))


def _proj(h, cos, sin, w):
    n = h.shape[0]
    tm = _pick_tile(cos.shape[0], _pick_tile(n, 512))
    nrb = cos.shape[0] // tm
    row = lambda wd_: pl.BlockSpec((tm, wd_), lambda i: (i, 0))
    rope = pl.BlockSpec((tm, ROPE_PAD), lambda i: (i % nrb, 0))
    wmla_w = Q_LORA + KV_LORA + 2 * ROPE_PAD
    return pl.pallas_call(
        _proj_kernel,
        grid=(n // tm,),
        in_specs=[row(D_MODEL), rope, rope,
                  _const_spec((D_MODEL, wmla_w)), _const_spec((D_MODEL, RW_COLS)),
                  _const_spec((D_MODEL, 2 * D_MODEL)), _const_spec((1, Q_LORA)), _const_spec((1, KV_LORA)),
                  _const_spec((Q_LORA, MLA_HEADS * QK_NOPE)), _const_spec((Q_LORA, MLA_HEADS * QK_ROPE)),
                  _const_spec((Q_LORA, MLA_HEADS * QK_ROPE)),
                  _const_spec((MLA_HEADS * QK_ROPE, MLA_HEADS * ROPE_PAD)),
                  _const_spec((MLA_HEADS // 2, 2 * QK_NOPE, 2 * KV_LORA))],
        out_specs=[pl.BlockSpec((MLA_HEADS, tm, QK_WIDTH), lambda i: (0, i, 0)),
                   row(QK_WIDTH), row(KV_LORA), row(QK_ROPE), row(RW_COLS), row(2 * D_MODEL)],
        out_shape=[jax.ShapeDtypeStruct((MLA_HEADS, n, QK_WIDTH), BF16),
                   jax.ShapeDtypeStruct((n, QK_WIDTH), BF16),
                   jax.ShapeDtypeStruct((n, KV_LORA), F32),
                   jax.ShapeDtypeStruct((n, QK_ROPE), F32),
                   jax.ShapeDtypeStruct((n, RW_COLS), F32),
                   jax.ShapeDtypeStruct((n, 2 * D_MODEL), F32)],
        compiler_params=_params(("parallel",)),
        name="proj",
    )(h, cos, sin, w["w_mla"], w["w_rw"], w["w_gate"], w["q_norm"], w["kv_norm"],
      w["w_nope"], w["w_pe"], w["w_pe_rot"], w["rope_place"], w["w_uk_bd"])


def _lane_tile(x, n):
    if n <= LANES:
        return x[:, :n]
    return jnp.concatenate([x] * (n // LANES), axis=1)


def _softmax_update(s, v, m_ref, l_ref, acc_ref, rows):
    m_prev = m_ref[rows, :]
    m_new = jnp.maximum(m_prev, jnp.max(s, axis=-1, keepdims=True))
    alpha = jnp.exp(m_prev - m_new)
    p = jnp.exp(s - _lane_tile(m_new, s.shape[1]))
    l_ref[rows, :] = alpha * l_ref[rows, :] + jnp.sum(p, axis=-1, keepdims=True)
    pb = p.astype(BF16)
    if isinstance(v, (list, tuple)):
        off, pv = 0, None
        for piece in v:
            part = _dot(pb[:, off:off + piece.shape[0]], piece)
            pv = part if pv is None else pv + part
            off += piece.shape[0]
    else:
        pv = _dot(pb, v)
    acc_ref[rows, :] = _lane_tile(alpha, KV_LORA) * acc_ref[rows, :] + pv
    m_ref[rows, :] = m_new


def _softmax_update_groups(scores, v, m_ref, l_ref, acc_ref, row_groups):
    m_prev = [m_ref[rows, :] for rows in row_groups]
    m_new = [jnp.maximum(mp, jnp.max(s, axis=-1, keepdims=True)) for mp, s in zip(m_prev, scores)]
    alpha = [jnp.exp(mp - mn) for mp, mn in zip(m_prev, m_new)]
    p = [jnp.exp(s - _lane_tile(mn, s.shape[1])) for s, mn in zip(scores, m_new)]
    for rows, al, pg in zip(row_groups, alpha, p):
        l_ref[rows, :] = al * l_ref[rows, :] + jnp.sum(pg, axis=-1, keepdims=True)
    pv = [_dot(pg.astype(BF16), v) for pg in p]
    for rows, al, pvg, mn in zip(row_groups, alpha, pv, m_new):
        acc_ref[rows, :] = _lane_tile(al, KV_LORA) * acc_ref[rows, :] + pvg
        m_ref[rows, :] = mn


def _softmax_init(m_ref, l_ref, acc_ref):
    m_ref[...] = jnp.full(m_ref.shape, NEG_INF, F32)
    l_ref[...] = jnp.zeros(l_ref.shape, F32)
    acc_ref[...] = jnp.zeros(acc_ref.shape, F32)


def _softmax_result(l_ref, acc_ref):
    return acc_ref[...] / _lane_tile(l_ref[...], KV_LORA)


def _mla_out(o_lat, wuv_ref, wo_ref, rows):
    heads = [_dot(o_lat[hh * rows:(hh + 1) * rows].astype(BF16), wuv_ref[hh]).astype(BF16)
             for hh in range(MLA_HEADS)]
    return _dot(jnp.concatenate(heads, axis=1), wo_ref[...])


ATTN_SPLIT = 2


def _attn_prompt_kernel(q_ref, kv_ref, wuv_ref, wo_ref, o_ref, m_ref, l_ref, acc_ref, *, tq, tk):
    i = pl.program_id(1)
    hs = MLA_HEADS // ATTN_SPLIT
    sub = hs * tq
    _softmax_init(m_ref, l_ref, acc_ref)
    n_full = (i * tq) // tk

    def chunks(starts, diagonal_last):
        kvs = [kv_ref[pl.ds(k0, tk), :] for k0 in starts]
        if diagonal_last:
            delta = (lax.broadcasted_iota(jnp.int32, (sub, tk), 1)
                     - lax.broadcasted_iota(jnp.int32, (sub, tk), 0) % tq)
            visible = delta <= i * tq - starts[-1]
        row_groups = [pl.ds(g * sub, sub) for g in range(ATTN_SPLIT)]
        qs = [q_ref[g * hs:(g + 1) * hs].reshape(sub, QK_WIDTH) for g in range(ATTN_SPLIT)]
        scores = [[_dot_nt(q, kv) for q in qs] for kv in kvs]
        if diagonal_last:
            scores[-1] = [jnp.where(visible, s, NEG_INF) for s in scores[-1]]
        for kv, s_groups in zip(kvs, scores):
            _softmax_update_groups(s_groups, kv[:, :KV_LORA], m_ref, l_ref, acc_ref, row_groups)

    def pair_step(j, carry):
        k0 = pl.multiple_of(2 * j * tk, 2 * tk)
        chunks([k0, pl.multiple_of(k0 + tk, tk)], False)
        return carry

    lax.fori_loop(0, n_full // 2, pair_step, 0)
    k_diag = pl.multiple_of(n_full * tk, tk)

    @pl.when(n_full % 2 == 1)
    def _():
        chunks([pl.multiple_of(k_diag - tk, tk), k_diag], True)

    @pl.when(n_full % 2 == 0)
    def _():
        chunks([k_diag], True)

    o_ref[...] = _mla_out(_softmax_result(l_ref, acc_ref), wuv_ref, wo_ref, tq)


def _attn_prompt(q, kv, wuv, wo, bsz, t):
    tq = _pick_tile(t, 256)
    tk = _pick_tile(t, 512)
    nq = t // tq
    rows = MLA_HEADS * tq
    kern = functools.partial(_attn_prompt_kernel, tq=tq, tk=tk)
    return pl.pallas_call(
        kern,
        grid=(bsz, nq),
        in_specs=[pl.BlockSpec((MLA_HEADS, tq, QK_WIDTH), lambda b, i: (0, b * nq + i, 0)),
                  pl.BlockSpec((t, QK_WIDTH), lambda b, i: (b, 0)),
                  _const_spec((MLA_HEADS, KV_LORA, V_DIM)), _const_spec((MLA_HEADS * V_DIM, D_MODEL))],
        out_specs=pl.BlockSpec((tq, D_MODEL), lambda b, i: (b * nq + i, 0)),
        out_shape=jax.ShapeDtypeStruct((bsz * t, D_MODEL), F32),
        scratch_shapes=[pltpu.VMEM((rows, LANES), F32), pltpu.VMEM((rows, LANES), F32),
                        pltpu.VMEM((rows, KV_LORA), F32)],
        compiler_params=_params(("parallel", "arbitrary")),
        name="attn_prompt",
    )(q, kv, wuv, wo)


SAMPLE_SLOTS = 4
SAMPLE_PARTS = 2

def _attn_sample_kernel(pt_ref, q_ref, kvn_ref, cc_hbm, ck_hbm, o_ref,
                        cbuf, kbuf, csem, ksem, m_ref, l_ref, acc_ref, *, pgc, n_pages, tnew):
    b = pl.program_id(0)
    nch = n_pages // pgc
    ahead = SAMPLE_SLOTS - 2
    page_rows = cbuf.shape[2]

    def chunk_copies(seq, ch):
        slot = ch % SAMPLE_SLOTS
        out = []
        for p in range(pgc):
            page = pt_ref[seq * n_pages + ch * pgc + p]
            out.append(pltpu.make_async_copy(cc_hbm.at[page], cbuf.at[slot, p], csem.at[slot]))
            out.append(pltpu.make_async_copy(ck_hbm.at[page], kbuf.at[slot, p], ksem.at[slot]))
        return out

    def start_chunk(ch):
        if ch < nch:
            for cp in chunk_copies(b, ch):
                cp.start()
        else:
            @pl.when(b + 1 < pl.num_programs(0))
            def _():
                for cp in chunk_copies(b + 1, ch - nch):
                    cp.start()

    @pl.when(b == 0)
    def _():
        for ch in range(ahead):
            for cp in chunk_copies(b, ch):
                cp.start()

    _softmax_init(m_ref, l_ref, acc_ref)
    q = q_ref[...]
    rows = q.shape[0]
    all_rows = pl.ds(0, rows)
    q_lat = q[:, :KV_LORA]
    q_pe = q[:, KV_LORA:KV_LORA + QK_ROPE]

    scored = None
    for ch in range(nch):
        start_chunk(ch + ahead)
        for cp in chunk_copies(b, ch):
            cp.wait()
        slot = ch % SAMPLE_SLOTS
        part = pgc // SAMPLE_PARTS
        cs = [cbuf[slot, h * part:(h + 1) * part].reshape(part * page_rows, KV_LORA).astype(BF16)
              for h in range(SAMPLE_PARTS)]
        kt = jnp.concatenate([kbuf[slot, p] for p in range(pgc)], axis=1).astype(BF16)
        s = jnp.concatenate([_dot_nt(q_lat, c) for c in cs], axis=1) + _dot(q_pe, kt)
        if scored is not None:
            _softmax_update(scored[0], scored[1], m_ref, l_ref, acc_ref, all_rows)
        scored = (s, cs)
    _softmax_update(scored[0], scored[1], m_ref, l_ref, acc_ref, all_rows)

    kvn = kvn_ref[...].astype(BF16)
    sn = _dot_nt(q, kvn)
    qt = lax.broadcasted_iota(jnp.int32, (rows, tnew), 0) % tnew
    kt_ = lax.broadcasted_iota(jnp.int32, (rows, tnew), 1)
    sn = jnp.where(kt_ <= qt, sn, NEG_INF)
    _softmax_update(sn, kvn[:, :KV_LORA], m_ref, l_ref, acc_ref, all_rows)
    o_ref[...] = _softmax_result(l_ref, acc_ref)


def _attn_sample(page_table, q, kvn, cache_c, cache_kt):
    nb, n_pages = page_table.shape
    tnew = kvn.shape[1]
    rows = q.shape[1]
    page = cache_c.shape[1]
    pgc = _pick_tile(n_pages, 32)
    assert (n_pages // pgc) % SAMPLE_SLOTS == 0, "ring slots are static per chunk, also across sequences"
    kern = functools.partial(_attn_sample_kernel, pgc=pgc, n_pages=n_pages, tnew=tnew)
    grid_spec = pltpu.PrefetchScalarGridSpec(
        num_scalar_prefetch=1,
        grid=(nb,),
        in_specs=[pl.BlockSpec((None, rows, QK_WIDTH), lambda b, pt: (b, 0, 0)),
                  pl.BlockSpec((None, tnew, QK_WIDTH), lambda b, pt: (b, 0, 0)),
                  pl.BlockSpec(memory_space=pl.ANY), pl.BlockSpec(memory_space=pl.ANY)],
        out_specs=pl.BlockSpec((None, rows, KV_LORA), lambda b, pt: (b, 0, 0)),
        scratch_shapes=[pltpu.VMEM((SAMPLE_SLOTS, pgc, page, KV_LORA), F32),
                        pltpu.VMEM((SAMPLE_SLOTS, pgc, QK_ROPE, page), F32),
                        pltpu.SemaphoreType.DMA((SAMPLE_SLOTS,)), pltpu.SemaphoreType.DMA((SAMPLE_SLOTS,)),
                        pltpu.VMEM((rows, LANES), F32), pltpu.VMEM((rows, LANES), F32),
                        pltpu.VMEM((rows, KV_LORA), F32)],
    )
    return pl.pallas_call(
        kern,
        grid_spec=grid_spec,
        out_shape=jax.ShapeDtypeStruct((nb, rows, KV_LORA), F32),
        compiler_params=_params(("arbitrary",)),
        name="attn_sample",
    )(page_table.reshape(-1), q, kvn, cache_c, cache_kt)


def _mla_out_kernel(ol_ref, wuv_ref, wo_ref, o_ref):
    tm = ol_ref.shape[1]
    o_ref[...] = _mla_out(ol_ref[...].reshape(MLA_HEADS * tm, KV_LORA), wuv_ref, wo_ref, tm)


def _mla_out_call(o_lat, wuv, wo):
    n = o_lat.shape[1]
    tm = _pick_tile(n, 256)
    return pl.pallas_call(
        _mla_out_kernel,
        grid=(n // tm,),
        in_specs=[pl.BlockSpec((MLA_HEADS, tm, KV_LORA), lambda i: (0, i, 0)),
                  _const_spec((MLA_HEADS, KV_LORA, V_DIM)), _const_spec((MLA_HEADS * V_DIM, D_MODEL))],
        out_specs=pl.BlockSpec((tm, D_MODEL), lambda i: (i, 0)),
        out_shape=jax.ShapeDtypeStruct((n, D_MODEL), F32),
        compiler_params=_params(("parallel",)),
        name="mla_out",
    )(o_lat, wuv, wo)


def _split_dot(x, w_bf, pieces):
    out = None
    rem = x
    for _ in range(pieces):
        hi = rem.astype(BF16)
        part = _dot(hi, w_bf)
        out = part if out is None else out + part
        rem = rem - hi.astype(F32)
    return out


def _split_dot_lhs(tri_bf, x):
    out = None
    rem = x
    for _ in range(3):
        hi = rem.astype(BF16)
        part = _dot(tri_bf, hi)
        out = part if out is None else out + part
        rem = rem - hi.astype(F32)
    return out


def _expand_heads(x, hg):
    lane_head = lax.broadcasted_iota(jnp.int32, x.shape, 1) // RW_HEAD
    zero = jnp.zeros((), x.dtype)
    return jnp.concatenate([jnp.where(lane_head == hh, x, zero) for hh in range(hg)], axis=0)


def _fold_heads(x, hg):
    c = x.shape[0] // hg
    out = x[0:c]
    for hh in range(1, hg):
        out = out + x[hh * c:(hh + 1) * c]
    return out


def _rwkv_kernel(u_ref, sh_ref, s0_ref, mu_ref, w0_ref, w2a2_ref, a0_ref, g2_ref, kkw_ref, ka_ref, rk_ref,
                 lnw_ref, lnb_ref, ones_ref, wo_ref,
                 ob_ref, sout_ref,
                 state_ref, prev_ref, p_ref, mk_ref, *, c, hg, nseq):
    t = pl.program_id(1)
    tt = u_ref.shape[1]
    nc = tt // c
    ng = RW_HEADS // hg
    w = hg * RW_HEAD
    r_ = hg * c

    blockmask = (lax.broadcasted_iota(jnp.int32, (w, w), 0) // RW_HEAD
                 == lax.broadcasted_iota(jnp.int32, (w, w), 1) // RW_HEAD)

    @pl.when(t == 0)
    def _():
        prev_ref[...] = sh_ref[...]
        for s in range(nseq):
            for g in range(ng):
                s0g = s0_ref[s, :, g * w:(g + 1) * w]
                state_ref[s, g] = jnp.where(blockmask, jnp.concatenate([s0g] * hg, axis=0), 0.0)

    u = u_ref[...].reshape(nseq * tt, RW_COLS)
    row = lax.broadcasted_iota(jnp.int32, u.shape, 0)
    u_prev = pltpu.roll(u, 1, 0)
    for s in range(nseq):
        u_prev = jnp.where(row == s * tt, prev_ref[s], u_prev)
        prev_ref[s] = u[(s + 1) * tt - 1:(s + 1) * tt, :]
    us = u + (u_prev - u) * mu_ref[...]
    r = us[:, 0:RW_DIM]
    k = us[:, RW_DIM:2 * RW_DIM]
    v = us[:, 2 * RW_DIM:3 * RW_DIM]
    o1 = 3 * RW_DIM
    wa = us[:, o1:o1 + DECAY_LORA + AAA_LORA]
    gd = us[:, o1 + DECAY_LORA + AAA_LORA:]
    lane = lax.broadcasted_iota(jnp.int32, wa.shape, 1)
    wa = jnp.where(lane < DECAY_LORA, jnp.tanh(wa), wa)
    lo = _dot(wa.astype(BF16), w2a2_ref[...])
    w_raw = -jax.nn.softplus(-(w0_ref[...] + lo[:, :RW_DIM])) - 0.5
    logw = -jnp.exp(w_raw)
    a = jax.nn.sigmoid(a0_ref[...] + lo[:, RW_DIM:])
    g = _dot(jax.nn.sigmoid(gd).astype(BF16), g2_ref[...])
    ones_bd = ones_ref[...]
    kk = k * kkw_ref[...]
    kk = kk / jnp.maximum(jnp.sqrt(_split_dot(kk * kk, ones_bd, 1)), 1e-12)
    k2 = k * (1.0 + (a - 1.0) * ka_ref[...])

    tri = (lax.broadcasted_iota(jnp.int32, (c, c), 1) <= lax.broadcasted_iota(jnp.int32, (c, c), 0)).astype(BF16)
    rr = lax.broadcasted_iota(jnp.int32, (r_, r_), 0)
    cc = lax.broadcasted_iota(jnp.int32, (r_, r_), 1)
    strict_bd = (rr // c == cc // c) & (cc % c < rr % c)
    eye = (rr == cc).astype(F32)
    incl_wide = (lax.broadcasted_iota(jnp.int32, (c, r_), 1) % c
                 <= lax.broadcasted_iota(jnp.int32, (c, r_), 0))

    inst = [(s, ci, g_) for s in range(nseq) for ci in range(nc) for g_ in range(ng)]
    number = {key: idx for idx, key in enumerate(inst)}
    pre = {}
    for s in range(nseq):
        for ci in range(nc):
            rows = slice(s * tt + ci * c, s * tt + (ci + 1) * c)
            lw = logw[rows]
            cs = _split_dot_lhs(tri, lw)
            e_cur = jnp.exp(cs)
            e_inv = jnp.exp(-cs)
            kkc = kk[rows]
            at = (-kkc * jnp.exp(cs - lw)).astype(BF16)
            bt = (kkc * a[rows] * e_inv).astype(BF16)
            kt = (k2[rows] * e_inv).astype(BF16)
            rt = (r[rows] * e_cur).astype(BF16)
            vb = v[rows].astype(BF16)
            for g_ in range(ng):
                sl = slice(g_ * w, (g_ + 1) * w)
                pre[s, ci, g_] = dict(a=at[:, sl], b=bt[:, sl], k=kt[:, sl], r=rt[:, sl], v=vb[:, sl],
                                      p_end=e_cur[c - 1:c, sl])
    for key in inst:
        z = pre[key]
        ax, bx, kx = (_expand_heads(z[name], hg) for name in ("a", "b", "k"))
        z["n1"] = _dot_nt(jnp.concatenate([ax, z["r"]], axis=0), jnp.concatenate([bx, kx], axis=0))
    for key in inst:
        z = pre[key]
        n1 = z.pop("n1")
        m_ab = jnp.where(strict_bd, n1[:r_, :r_], 0.0)
        p_ref[number[key]] = eye + m_ab
        mk_ref[number[key]] = m_ab.astype(BF16)
        z["m_ak"] = jnp.where(strict_bd, n1[:r_, r_:], 0.0).astype(BF16)
        z["a_rb"] = jnp.where(incl_wide, n1[r_:, :r_], 0.0).astype(BF16)
        z["a_rk"] = jnp.where(incl_wide, n1[r_:, r_:], 0.0).astype(BF16)
    for key in inst:
        z = pre[key]
        vx = _expand_heads(z["v"], hg)
        z["wv"] = _dot(z.pop("m_ak"), vx)
        z["yv"] = _dot(z.pop("a_rk"), vx)

    def inverse_level(_, carry):
        squares = []
        for idx in range(len(inst)):
            mk = mk_ref[idx]
            squares.append(_dot(mk, mk).astype(BF16))
        for idx, mk2 in enumerate(squares):
            p = p_ref[idx]
            p_ref[idx] = p + _dot(p.astype(BF16), mk2)
            mk_ref[idx] = mk2
        return carry

    levels = max(c.bit_length() - 2, 0)
    lax.fori_loop(0, levels, inverse_level, 0)

    y_part = {}
    for ci in range(nc):
        keys = [(s, ci, g_) for s in range(nseq) for g_ in range(ng)]
        state = {key: state_ref[key[0], key[2]] for key in keys}
        sa = {key: _dot_nt(jnp.concatenate([pre[key]["a"], pre[key]["r"]], axis=0), state[key].astype(BF16))
              for key in keys}
        ux = {key: _dot(p_ref[number[key]].astype(BF16),
                        (_expand_heads(sa[key][:c], hg) + pre[key]["wv"]).astype(BF16))
              for key in keys}
        for key in keys:
            z = pre[key]
            y_part[key] = sa[key][c:] + _dot(z["a_rb"], ux[key].astype(BF16)) + z["yv"]
        ds = {key: _dot_tn(jnp.concatenate([_fold_heads(ux[key], hg).astype(BF16), pre[key]["v"]], axis=0),
                           jnp.concatenate([pre[key]["b"], pre[key]["k"]], axis=0))
              for key in keys}
        for key in keys:
            state_ref[key[0], key[2]] = (state[key] + jnp.where(blockmask, ds[key], 0.0)) * pre[key]["p_end"]
    y_rows = []
    for s in range(nseq):
        for ci in range(nc):
            ys = [y_part[s, ci, g_] for g_ in range(ng)]
            y_rows.append(ys[0] if ng == 1 else jnp.concatenate(ys, axis=1))
    y = y_rows[0] if len(y_rows) == 1 else jnp.concatenate(y_rows, axis=0)

    inv_n = 1.0 / RW_HEAD
    mean = _split_dot(y, ones_bd, 2) * inv_n
    d = y - mean
    var = _split_dot(d * d, ones_bd, 1) * inv_n
    yn = d * lax.rsqrt(var + GN_EPS) * lnw_ref[...] + lnb_ref[...]
    bonus = _split_dot(r * k2 * rk_ref[...], ones_bd, 1) * v
    ob_ref[...] = _dot(((yn + bonus) * g).astype(BF16), wo_ref[...]).reshape(nseq, tt, D_MODEL)

    @pl.when(t == pl.num_programs(1) - 1)
    def _():
        for s in range(nseq):
            for g_ in range(ng):
                sout_ref[s, :, g_ * w:(g_ + 1) * w] = _fold_heads(state_ref[s, g_], hg)


RW_STEP_TOKENS = 512


def _rwkv(u_rw, shift_in, s0, w, bsz, t):
    if t >= 64:
        c, hg, nseq = 64, 4, _pick_tile(bsz, 8)
        tt = _pick_tile(t, max(c, RW_STEP_TOKENS // nseq))
    else:
        c, hg, nseq = t, RW_HEADS, _pick_tile(bsz, 8)
        tt = t
    nt = t // tt
    ng = RW_HEADS // hg
    wd_ = hg * RW_HEAD
    n_inst = nseq * (tt // c) * ng
    kern = functools.partial(_rwkv_kernel, c=c, hg=hg, nseq=nseq)
    vec = lambda n: _const_spec((1, n))
    o_b, s_new = pl.pallas_call(
        kern,
        grid=(bsz // nseq, nt),
        in_specs=[pl.BlockSpec((nseq, tt, RW_COLS), lambda b, i: (b, i, 0)),
                  pl.BlockSpec((nseq, 1, RW_COLS), lambda b, i: (b, 0, 0)),
                  pl.BlockSpec((nseq, RW_HEAD, RW_DIM), lambda b, i: (b, 0, 0)),
                  vec(RW_COLS), vec(RW_DIM), _const_spec((DECAY_LORA + AAA_LORA, 2 * RW_DIM)), vec(RW_DIM),
                  _const_spec((GATE_LORA, RW_DIM)), vec(RW_DIM), vec(RW_DIM), vec(RW_DIM), vec(RW_DIM),
                  vec(RW_DIM), _const_spec((RW_DIM, RW_DIM)), _const_spec((RW_DIM, D_MODEL))],
        out_specs=[pl.BlockSpec((nseq, tt, D_MODEL), lambda b, i: (b, i, 0)),
                   pl.BlockSpec((nseq, RW_HEAD, RW_DIM), lambda b, i: (b, 0, 0))],
        out_shape=[jax.ShapeDtypeStruct((bsz, t, D_MODEL), F32),
                   jax.ShapeDtypeStruct((bsz, RW_HEAD, RW_DIM), F32)],
        scratch_shapes=[pltpu.VMEM((nseq, ng, wd_, wd_), F32), pltpu.VMEM((nseq, 1, RW_COLS), F32),
                        pltpu.VMEM((n_inst, hg * c, hg * c), F32), pltpu.VMEM((n_inst, hg * c, hg * c), BF16)],
        compiler_params=_params(("parallel", "arbitrary")),
        name="rwkv",
    )(u_rw.reshape(bsz, t, RW_COLS), shift_in, s0, w["rw_mu"], w["rw_w0"], w["rw_w2a2"], w["rw_a0"], w["rw_g2"],
      w["rw_kk"], w["rw_ka"], w["rw_rk"], w["rw_ln_w"], w["rw_ln_b"], w["ones_bd"], w["w_o_rw"])
    return o_b.reshape(bsz * t, D_MODEL), s_new


def _merge_kernel(x1_ref, oa_ref, ob_ref, gate_ref, wout_ref, n2_ref, wg_ref, wu_ref, wd_ref, nf_ref, y_ref):
    mixed = (gate_ref[:, :D_MODEL] * oa_ref[...] + gate_ref[:, D_MODEL:] * ob_ref[...]).astype(BF16)
    x2 = x1_ref[...] + _dot(mixed, wout_ref[...])
    hn = _rmsnorm(x2, n2_ref[...]).astype(BF16)
    x3 = x2 + 0.5 * _swiglu(hn, wg_ref, wu_ref, wd_ref)
    y_ref[...] = _rmsnorm(x3, nf_ref[...])


def _merge(x1, oa, ob, gate, wout, n2, wg, wu, wd, nf):
    n = x1.shape[0]
    tm = _pick_tile(n, 256)
    row = lambda w_: pl.BlockSpec((tm, w_), lambda i: (i, 0))
    return pl.pallas_call(
        _merge_kernel,
        grid=(n // tm,),
        in_specs=[row(D_MODEL), row(D_MODEL), row(D_MODEL), row(2 * D_MODEL),
                  _const_spec((D_MODEL, D_MODEL)), _const_spec((1, D_MODEL)), _const_spec((D_MODEL, D_FF)),
                  _const_spec((D_MODEL, D_FF)), _const_spec((D_FF, D_MODEL)), _const_spec((1, D_MODEL))],
        out_specs=row(D_MODEL),
        out_shape=jax.ShapeDtypeStruct((n, D_MODEL), F32),
        compiler_params=_params(("parallel",)),
        name="merge_ffn2",
    )(x1, oa, ob, gate, wout, n2, wg, wu, wd, nf)


def _rope_tables(pos):
    inv = 1.0 / (ROPE_THETA ** (jnp.arange(0, QK_ROPE, 2, dtype=F32) / QK_ROPE))
    ang = pos.astype(F32)[:, None] * inv[None, :]
    cos, sin = jnp.cos(ang), jnp.sin(ang)
    reps = 2 * ROPE_PAD // QK_ROPE
    return jnp.concatenate([cos] * reps, axis=1), jnp.concatenate([sin] * reps, axis=1)


def _rot_cols(wm):
    half = QK_ROPE // 2
    return jnp.concatenate([-wm[..., half:], wm[..., :half]], axis=-1)


def _pad_cols(wm, width):
    return jnp.pad(wm, [(0, 0)] * (wm.ndim - 1) + [(0, width - wm.shape[-1])])


def _prep_weights(p):
    w_in = p["w_in"]
    w_kpe = w_in[:, Q_LORA + KV_LORA:MLA_COLS]
    w_mla = jnp.concatenate([w_in[:, :Q_LORA + KV_LORA], _pad_cols(w_kpe, ROPE_PAD),
                             _pad_cols(_rot_cols(w_kpe), ROPE_PAD)], axis=1)
    w_qb = p["w_qb"].reshape(Q_LORA, MLA_HEADS, QK_NOPE + QK_ROPE)
    w_nope = w_qb[:, :, :QK_NOPE].reshape(Q_LORA, MLA_HEADS * QK_NOPE)
    w_pe = w_qb[:, :, QK_NOPE:]
    w_uk = jnp.transpose(p["w_uk"], (1, 2, 0)).reshape(MLA_HEADS // 2, 2, QK_NOPE, KV_LORA)
    same_head = jnp.eye(2, dtype=bool)[None, :, None, :, None]
    w_uk_bd = jnp.where(same_head, w_uk[:, :, :, None, :], 0.0).reshape(MLA_HEADS // 2, 2 * QK_NOPE, 2 * KV_LORA)
    zeros = jnp.zeros((DECAY_LORA, RW_DIM), F32)
    w2a2 = jnp.concatenate([jnp.concatenate([p["rw_w2"], zeros], axis=1),
                            jnp.concatenate([zeros, p["rw_a2"]], axis=1)], axis=0)
    head = jnp.arange(RW_DIM) // RW_HEAD
    rope_lane = jnp.arange(MLA_HEADS * QK_ROPE)
    rope_src = (rope_lane // QK_ROPE) * ROPE_PAD + rope_lane % QK_ROPE
    row = lambda v: v.reshape(1, -1)
    return {
        "w_mla": w_mla.astype(BF16),
        "w_rw": w_in[:, MLA_COLS:MLA_COLS + RW_COLS].astype(BF16),
        "w_gate": w_in[:, MLA_COLS + RW_COLS:].astype(BF16),
        "q_norm": row(p["q_norm"]), "kv_norm": row(p["kv_norm"]),
        "w_nope": w_nope.astype(BF16),
        "w_pe": w_pe.reshape(Q_LORA, MLA_HEADS * QK_ROPE).astype(BF16),
        "w_pe_rot": _rot_cols(w_pe).reshape(Q_LORA, MLA_HEADS * QK_ROPE).astype(BF16),
        "rope_place": (rope_src[:, None] == jnp.arange(MLA_HEADS * ROPE_PAD)[None, :]).astype(BF16),
        "w_uk_bd": w_uk_bd.astype(BF16),
        "w_uv": jnp.transpose(p["w_uv"], (1, 0, 2)).astype(BF16),
        "w_o_mla": p["w_o_mla"].astype(BF16),
        "rw_mu": row(p["rw_mu"]), "rw_w0": row(p["rw_w0"]), "rw_w2a2": w2a2.astype(BF16),
        "rw_a0": row(p["rw_a0"]), "rw_g2": p["rw_g2"].astype(BF16), "rw_kk": row(p["rw_kk"]),
        "rw_ka": row(p["rw_ka"]), "rw_rk": row(p["rw_rk"]), "rw_ln_w": row(p["rw_ln_w"]),
        "rw_ln_b": row(p["rw_ln_b"]),
        "ones_bd": (head[:, None] == head[None, :]).astype(BF16),
        "w_o_rw": p["w_o_rw"].astype(BF16),
        "w_out": p["w_out"].astype(BF16),
        "ffn1_norm": row(p["ffn1_norm"]), "mix_norm": row(p["mix_norm"]), "ffn2_norm": row(p["ffn2_norm"]),
        "ffn1_wg": p["ffn1_wg"].astype(BF16), "ffn1_wu": p["ffn1_wu"].astype(BF16),
        "ffn1_wd": p["ffn1_wd"].astype(BF16),
        "ffn2_wg": p["ffn2_wg"].astype(BF16), "ffn2_wu": p["ffn2_wu"].astype(BF16),
        "ffn2_wd": p["ffn2_wd"].astype(BF16),
    }


def _state_to_lanes(s):
    b = s.shape[0]
    return jnp.transpose(s, (0, 2, 1, 3)).reshape(b, RW_HEAD, RW_DIM)


def _state_from_lanes(s):
    b = s.shape[0]
    return jnp.transpose(s.reshape(b, RW_HEAD, RW_HEADS, RW_HEAD), (0, 2, 1, 3))


def _layer(x, cos, sin, shift_in, s0, w, final_norm, attend):
    bsz, t, _ = x.shape
    n = bsz * t
    x1, h = _ffn1(x.reshape(n, D_MODEL), w["ffn1_norm"], w["ffn1_wg"], w["ffn1_wu"], w["ffn1_wd"], w["mix_norm"])
    q, kv, ckv, kpe, u_rw, gate = _proj(h, cos, sin, w)
    o_a = attend(q, kv)
    o_b, s_new = _rwkv(u_rw, shift_in.reshape(bsz, 1, RW_COLS), _state_to_lanes(s0), w, bsz, t)
    y = _merge(x1, o_a, o_b, gate, w["w_out"], w["ffn2_norm"], w["ffn2_wg"], w["ffn2_wu"], w["ffn2_wd"],
               final_norm.reshape(1, D_MODEL))
    return (y.reshape(bsz, t, D_MODEL), ckv.reshape(bsz, t, KV_LORA), kpe.reshape(bsz, t, QK_ROPE),
            _state_from_lanes(s_new), u_rw.reshape(bsz, t, RW_COLS)[:, -1])


def kernel(x_prompt, x_sample, cache_ckv, cache_kpe, state_wkv, state_shift, page_table, ffn1_norm, ffn1_wg,
           ffn1_wu, ffn1_wd, mix_norm, w_in, q_norm, kv_norm, w_qb, w_uk, w_uv, w_o_mla, rw_mu, rw_w0, rw_w2,
           rw_a0, rw_a2, rw_g2, rw_kk, rw_ka, rw_rk, rw_ln_w, rw_ln_b, w_o_rw, w_out, ffn2_norm, ffn2_wg,
           ffn2_wu, ffn2_wd, final_norm):
    depth = w_in.shape[0]
    assert depth == 1, "the final norm is fused into the layer's last stage"
    layer_params = dict(ffn1_norm=ffn1_norm, ffn1_wg=ffn1_wg, ffn1_wu=ffn1_wu, ffn1_wd=ffn1_wd, mix_norm=mix_norm,
                        w_in=w_in, q_norm=q_norm, kv_norm=kv_norm, w_qb=w_qb, w_uk=w_uk, w_uv=w_uv,
                        w_o_mla=w_o_mla, rw_mu=rw_mu, rw_w0=rw_w0, rw_w2=rw_w2, rw_a0=rw_a0, rw_a2=rw_a2,
                        rw_g2=rw_g2, rw_kk=rw_kk, rw_ka=rw_ka, rw_rk=rw_rk, rw_ln_w=rw_ln_w, rw_ln_b=rw_ln_b,
                        w_o_rw=w_o_rw, w_out=w_out, ffn2_norm=ffn2_norm, ffn2_wg=ffn2_wg, ffn2_wu=ffn2_wu,
                        ffn2_wd=ffn2_wd)
    w = _prep_weights({k: v[0] for k, v in layer_params.items()})

    bp, tp, _ = x_prompt.shape
    bs, ts, _ = x_sample.shape
    n_pages = page_table.shape[1]
    past_len = n_pages * cache_ckv.shape[2]

    cos_p, sin_p = _rope_tables(jnp.arange(tp, dtype=jnp.int32))
    attend_p = lambda q, kv: _attn_prompt(q, kv, w["w_uv"], w["w_o_mla"], bp, tp)
    out_p = _layer(x_prompt, cos_p, sin_p, jnp.zeros((bp, RW_COLS), F32),
                   jnp.zeros((bp, RW_HEADS, RW_HEAD, RW_HEAD), F32), w, final_norm, attend_p)

    n_s = bs * ts
    tile_s = _pick_tile(n_s, 512)
    pos_s = past_len + jnp.arange(tile_s, dtype=jnp.int32) % ts
    cos_s, sin_s = _rope_tables(pos_s)

    def attend_s(q, kv):
        q_b = jnp.transpose(q.reshape(MLA_HEADS, bs, ts, QK_WIDTH), (1, 0, 2, 3)).reshape(bs, MLA_HEADS * ts, QK_WIDTH)
        kv_new = kv.astype(F32).reshape(bs, ts, QK_WIDTH)
        o_lat = _attn_sample(page_table, q_b, kv_new, cache_ckv[0], jnp.swapaxes(cache_kpe[0], 1, 2))
        o_lat = jnp.transpose(o_lat.reshape(bs, MLA_HEADS, ts, KV_LORA), (1, 0, 2, 3)).reshape(MLA_HEADS, n_s, KV_LORA)
        return _mla_out_call(o_lat, w["w_uv"], w["w_o_mla"])

    out_s = _layer(x_sample, cos_s, sin_s, state_shift[0], state_wkv[0], w, final_norm, attend_s)

    y_p, ckv_p, kpe_p, wkv_p, sh_p = out_p
    y_s, ckv_s, kpe_s, wkv_s, sh_s = out_s
    stack = lambda z: z[None]
    return (y_p, y_s, stack(ckv_p), stack(kpe_p), stack(wkv_p), stack(sh_p),
            stack(ckv_s), stack(kpe_s), stack(wkv_s), stack(sh_s))
```
